```python
import math, functools
import jax, jax.numpy as jnp
from jax import lax
import numpy as np

D_MODEL = 2048
BATCH = 1
SEQ = 8192
DEPTH = 1
DEC_BATCH = 32
DEC_SEQ = 8
PAST_LEN = 16384
PAGE_SIZE = 128

HEAD_DIM = 128
MIX_WIDTH = D_MODEL
SB_WIDTH = MIX_WIDTH // 2
GDN_WIDTH = MIX_WIDTH - SB_WIDTH
N_SB_HEADS = SB_WIDTH // HEAD_DIM
N_GDN_HEADS = GDN_WIDTH // HEAD_DIM
SB_BLOCK = 128
SB_BIAS_INIT = -8.0
GDN_CHUNK = 64
CONV_W = 4
GDN_QKV = 3 * GDN_WIDTH
OFF_SK = SB_WIDTH
OFF_SV = 2 * SB_WIDTH
OFF_GQKV = 3 * SB_WIDTH
OFF_GZ = OFF_GQKV + GDN_QKV
OFF_GA = OFF_GZ + GDN_WIDTH
OFF_GB = OFF_GA + N_GDN_HEADS
IN_COLS = OFF_GB + N_GDN_HEADS
N_EXPERTS = 32
TOP_K = 4
D_FF = D_MODEL
SWIGLU_LIMIT = 7.0
SWIGLU_ALPHA = 1.702
MOE_BLOCK = 128
N_MOD = 6
EPS = 1e-6
POOL_NUM = 5
POOL_DEN = 4

kernel_name = 'hybrid_sb_gdn_moe_step'

F32 = jnp.float32


def rms_norm(x, g):
    xf = x.astype(F32)
    y = xf * lax.rsqrt(jnp.mean(xf * xf, axis=-1, keepdims=True) + EPS)
    return (y * g.astype(F32)).astype(x.dtype)


def l2_normalize(x):
    return x * lax.rsqrt(jnp.sum(x * x, axis=-1, keepdims=True) + EPS)


def modulate(h, shift, scale):
    return h * (1 + scale[:, None, :]) + shift[:, None, :]


def sb_key_block(q, qpos, k, v, kpos, bias, acc, lsum):
    z = jnp.einsum('bqhd,bkhd->bhqk', q.astype(F32), k.astype(F32)) * (HEAD_DIM ** -0.5)
    z = z + bias.astype(F32)[None, :, None, None]
    valid = kpos[None, :] < qpos[:, None]
    log_1mb = jnp.where(valid, jax.nn.log_sigmoid(-z), 0.0)
    later = jnp.concatenate([log_1mb[..., 1:], jnp.zeros_like(log_1mb[..., :1])], axis=-1)
    suffix = lax.cumsum(later, axis=3, reverse=True)
    log_a = jnp.where(valid, jax.nn.log_sigmoid(z) + suffix + lsum[..., None], -jnp.inf)
    acc = acc + jnp.einsum('bhqk,bkhd->bqhd', jnp.exp(log_a), v.astype(F32))
    return acc, lsum + jnp.sum(log_1mb, axis=-1)


def sba_prompt(q, k, v, bias):
    b, l, h, dh = q.shape
    nq = l // SB_BLOCK
    qb = jnp.moveaxis(q.reshape(b, nq, SB_BLOCK, h, dh), 1, 0)
    kpos = jnp.arange(l)

    def one_block(args):
        q_blk, i = args
        qpos = i * SB_BLOCK + jnp.arange(SB_BLOCK)
        acc0 = jnp.zeros((b, SB_BLOCK, h, dh), F32)
        ls0 = jnp.zeros((b, h, SB_BLOCK), F32)
        acc, _ = sb_key_block(q_blk, qpos, k, v, kpos, bias, acc0, ls0)
        return acc

    o = lax.map(one_block, (qb, jnp.arange(nq)))
    return jnp.moveaxis(o, 0, 1).reshape(b, l, h, dh)


def sba_paged(q, k, v, bias, k_pool, v_pool, page_table):
    nb, s, h, dh = q.shape
    n_pages = page_table.shape[1]
    qpos = n_pages * PAGE_SIZE + jnp.arange(s)
    acc, lsum = sb_key_block(q, qpos, k, v, qpos, bias, jnp.zeros((nb, s, h, dh), F32), jnp.zeros((nb, h, s), F32))

    def page_step(carry, xs):
        pt_col, p = xs
        kpos = p * PAGE_SIZE + jnp.arange(PAGE_SIZE)
        return sb_key_block(q, qpos, k_pool[pt_col], v_pool[pt_col], kpos, bias, carry[0], carry[1]), None

    (acc, _), _ = lax.scan(page_step, (acc, lsum), (page_table.T, jnp.arange(n_pages)), reverse=True)
    return acc


def gated_delta_chunked(q, k, v, log_decay, beta, s0):
    n, l, h, dk = q.shape
    dv = v.shape[-1]
    c = GDN_CHUNK
    nc = -(-l // c)
    pad = nc * c - l

    def to_chunks(t):
        t = jnp.pad(t, [(0, 0), (0, pad)] + [(0, 0)] * (t.ndim - 2))
        t = t.reshape((n, nc, c) + t.shape[2:])
        return jnp.moveaxis(t, 3, 1)

    q, k, v, g, beta = (to_chunks(t) for t in (q, k, v, log_decay, beta))
    q = q * (dk ** -0.5)
    gc = jnp.cumsum(g, axis=-1)
    idx = jnp.arange(c)
    causal = idx[:, None] >= idx[None, :]
    strict = idx[:, None] > idx[None, :]
    decay = jnp.exp(jnp.where(causal, gc[..., :, None] - gc[..., None, :], -jnp.inf))
    kb = k * beta[..., None]
    lmat = jnp.where(strict, jnp.einsum('...id,...jd->...ij', kb, k) * decay, 0.0)
    rhs = jnp.concatenate([v * beta[..., None], kb * jnp.exp(gc)[..., None]], axis=-1)
    sol = lax.linalg.triangular_solve(lmat + jnp.eye(c, dtype=F32), rhs, left_side=True, lower=True, unit_diagonal=True)
    u, w = sol[..., :dv], sol[..., dv:]
    qk = jnp.einsum('...id,...jd->...ij', q, k) * decay

    def step(s, xs):
        q_c, k_c, u_c, w_c, qk_c, g_c = xs
        v_new = u_c - w_c @ s
        o_c = (q_c * jnp.exp(g_c)[..., None]) @ s + qk_c @ v_new
        g_last = g_c[..., -1]
        s = s * jnp.exp(g_last)[..., None, None] + jnp.einsum('bhcd,bhce->bhde', k_c * jnp.exp(g_last[..., None] - g_c)[..., None], v_new)
        return s, o_c

    xs = tuple(jnp.moveaxis(t, 2, 0) for t in (q, k, u, w, qk, gc))
    s_fin, o = lax.scan(step, s0.astype(F32), xs)
    o = jnp.moveaxis(jnp.moveaxis(o, 0, 2), 1, 3).reshape(n, nc * c, h, dv)[:, :l]
    return o, s_fin


def moe_ffn(h, w_router, b_router, w_gu, b_gu, w_down, b_down):
    t, d = h.shape
    logits = (h @ w_router + b_router).astype(F32)
    top_logit, top_idx = lax.top_k(logits, TOP_K)
    gates = jax.nn.softmax(top_logit, axis=-1)
    n_slots = t * TOP_K
    flat_e = top_idx.reshape(-1)
    order = jnp.argsort(flat_e)
    sorted_e = flat_e[order]
    sorted_tok = order // TOP_K
    sorted_gate = gates.reshape(-1)[order]
    counts = jnp.bincount(flat_e, length=N_EXPERTS)
    padded = (counts + MOE_BLOCK - 1) // MOE_BLOCK * MOE_BLOCK
    pad_end = jnp.cumsum(padded)
    pad_start = pad_end - padded
    grp_start = jnp.cumsum(counts) - counts
    dest = pad_start[sorted_e] + jnp.arange(n_slots) - grp_start[sorted_e]
    n_blocks = -(-n_slots // MOE_BLOCK) + N_EXPERTS
    cap = n_blocks * MOE_BLOCK
    src_tok = jnp.full((cap,), t, jnp.int32).at[dest].set(sorted_tok)
    h_ext = jnp.concatenate([h, jnp.zeros((1, d), h.dtype)], axis=0)
    xb = h_ext[src_tok].reshape(n_blocks, MOE_BLOCK, d)
    block_e = jnp.minimum(jnp.searchsorted(pad_end, jnp.arange(n_blocks) * MOE_BLOCK, side='right'), N_EXPERTS - 1)

    def expert_block(args):
        x_blk, e = args
        gu = x_blk @ w_gu[e] + b_gu[e]
        gate = jnp.minimum(gu[:, :D_FF], SWIGLU_LIMIT)
        up = jnp.clip(gu[:, D_FF:], -SWIGLU_LIMIT, SWIGLU_LIMIT)
        act = (up + 1) * gate * jax.nn.sigmoid(gate * SWIGLU_ALPHA)
        return act @ w_down[e] + b_down[e]

    yb = lax.map(expert_block, (xb, block_e))
    y_slot = yb.reshape(cap, d)[dest].astype(F32)
    out = jnp.zeros((t, d), F32).at[sorted_tok].add(y_slot * sorted_gate[:, None])
    return out.astype(h.dtype)


def layer_forward(x, c, sba_fn, conv0, s0, w_ada, b_ada, norm1_g, norm2_g, w_in, sb_bias, conv_w, a_log, dt_bias,
                  gdn_norm_g, w_out, w_router, b_router, w_gu, b_gu, w_down, b_down):
    n, l, d = x.shape
    mods = jax.nn.silu(c) @ w_ada + b_ada
    sh1, sc1, gt1, sh2, sc2, gt2 = jnp.split(mods, N_MOD, axis=-1)
    hmix = modulate(rms_norm(x, norm1_g), sh1, sc1)
    proj = hmix @ w_in

    def heads(t):
        return t.reshape(n, l, -1, HEAD_DIM)

    sq = heads(proj[..., :OFF_SK])
    sk = heads(proj[..., OFF_SK:OFF_SV])
    sv = heads(proj[..., OFF_SV:OFF_GQKV])
    o_sb = sba_fn(sq, sk, sv, sb_bias)

    qkv_pre = proj[..., OFF_GQKV:OFF_GZ]
    xp = jnp.concatenate([conv0.astype(qkv_pre.dtype), qkv_pre], axis=1)
    qkv = jax.nn.silu(sum(xp[:, i:i + l] * conv_w[:, i] for i in range(CONV_W)))
    new_conv = xp[:, -(CONV_W - 1):]
    gq, gk, gv = jnp.split(qkv, 3, axis=-1)
    gq = l2_normalize(heads(gq).astype(F32))
    gk = l2_normalize(heads(gk).astype(F32))
    gv = heads(gv).astype(F32)
    gz = heads(proj[..., OFF_GZ:OFF_GA]).astype(F32)
    ga = proj[..., OFF_GA:OFF_GB].astype(F32)
    gb = proj[..., OFF_GB:].astype(F32)
    log_decay = -jnp.exp(a_log.astype(F32)) * jax.nn.softplus(ga + dt_bias.astype(F32))
    beta = jax.nn.sigmoid(gb)
    o_gdn, s_new = gated_delta_chunked(gq, gk, gv, log_decay, beta, s0)
    o_gdn = rms_norm(o_gdn, gdn_norm_g) * jax.nn.silu(gz)

    merged = jnp.concatenate([o_sb.reshape(n, l, SB_WIDTH).astype(x.dtype),
                              o_gdn.reshape(n, l, GDN_WIDTH).astype(x.dtype)], axis=-1)
    x = x + gt1[:, None, :] * (merged @ w_out)

    hffn = modulate(rms_norm(x, norm2_g), sh2, sc2)
    ffn = moe_ffn(hffn.reshape(n * l, d), w_router, b_router, w_gu, b_gu, w_down, b_down).reshape(n, l, d)
    x = x + gt2[:, None, :] * ffn
    return x, sk, sv, s_new, new_conv


def final_norm(y, c, w_ada_final, b_ada_final, norm_f_g):
    sh, sc = jnp.split(jax.nn.silu(c) @ w_ada_final + b_ada_final, 2, axis=-1)
    return modulate(rms_norm(y, norm_f_g), sh, sc)


def setup_inputs(seed: int = 0) -> dict:
    key = jax.random.key(seed)
    ks = jax.random.split(key, 32)
    n_pages = PAST_LEN // PAGE_SIZE
    n_phys = (DEC_BATCH * n_pages * POOL_NUM) // POOL_DEN

    def nrm(k, shape, s=1.0):
        return s * jax.random.normal(k, shape, F32)

    page_table = jax.random.permutation(ks[6], n_phys)[:DEC_BATCH * n_pages].reshape(DEC_BATCH, n_pages).astype(jnp.int32)
    dt = jnp.exp(jax.random.uniform(ks[15], (DEPTH, N_GDN_HEADS), F32, math.log(1e-3), math.log(1e-1)))
    return {
        'x_prompt': nrm(ks[0], (BATCH, SEQ, D_MODEL)),
        'x_sample': nrm(ks[1], (DEC_BATCH, DEC_SEQ, D_MODEL)),
        'cache_k': nrm(ks[2], (DEPTH, n_phys, PAGE_SIZE, N_SB_HEADS, HEAD_DIM)),
        'cache_v': nrm(ks[3], (DEPTH, n_phys, PAGE_SIZE, N_SB_HEADS, HEAD_DIM)),
        'state_gdn': nrm(ks[4], (DEPTH, DEC_BATCH, N_GDN_HEADS, HEAD_DIM, HEAD_DIM), 0.05),
        'state_conv': nrm(ks[5], (DEPTH, DEC_BATCH, CONV_W - 1, GDN_QKV)),
        'page_table': page_table,
        'c_prompt': nrm(ks[7], (BATCH, D_MODEL)),
        'c_sample': nrm(ks[8], (DEC_BATCH, D_MODEL)),
        'w_ada': nrm(ks[9], (DEPTH, D_MODEL, N_MOD * D_MODEL), 0.5 * D_MODEL ** -0.5),
        'b_ada': nrm(ks[10], (DEPTH, N_MOD * D_MODEL), 0.02),
        'norm1_g': 1.0 + nrm(ks[11], (DEPTH, D_MODEL), 0.02),
        'norm2_g': 1.0 + nrm(ks[12], (DEPTH, D_MODEL), 0.02),
        'w_in': nrm(ks[13], (DEPTH, D_MODEL, IN_COLS), D_MODEL ** -0.5),
        'sb_bias': SB_BIAS_INIT + nrm(ks[28], (DEPTH, N_SB_HEADS), 0.1),
        'conv_w': nrm(ks[14], (DEPTH, GDN_QKV, CONV_W), CONV_W ** -0.5),
        'a_log': jnp.log(jax.random.uniform(ks[16], (DEPTH, N_GDN_HEADS), F32, 1.0, 16.0)),
        'dt_bias': dt + jnp.log(-jnp.expm1(-dt)),
        'gdn_norm_g': 1.0 + nrm(ks[17], (DEPTH, HEAD_DIM), 0.02),
        'w_out': nrm(ks[18], (DEPTH, MIX_WIDTH, D_MODEL), MIX_WIDTH ** -0.5),
        'w_router': nrm(ks[19], (DEPTH, D_MODEL, N_EXPERTS), D_MODEL ** -0.5),
        'b_router': nrm(ks[20], (DEPTH, N_EXPERTS), 0.01),
        'w_gu': nrm(ks[21], (DEPTH, N_EXPERTS, D_MODEL, 2 * D_FF), D_MODEL ** -0.5),
        'b_gu': nrm(ks[22], (DEPTH, N_EXPERTS, 2 * D_FF), 0.01),
        'w_down': nrm(ks[23], (DEPTH, N_EXPERTS, D_FF, D_MODEL), D_FF ** -0.5),
        'b_down': nrm(ks[24], (DEPTH, N_EXPERTS, D_MODEL), 0.01),
        'w_ada_final': nrm(ks[25], (D_MODEL, 2 * D_MODEL), 0.5 * D_MODEL ** -0.5),
        'b_ada_final': nrm(ks[26], (2 * D_MODEL,), 0.02),
        'norm_f_g': 1.0 + nrm(ks[27], (D_MODEL,), 0.02),
    }


def reference(x_prompt, x_sample, cache_k, cache_v, state_gdn, state_conv, page_table, c_prompt, c_sample,
              w_ada, b_ada, norm1_g, norm2_g, w_in, sb_bias, conv_w, a_log, dt_bias, gdn_norm_g, w_out,
              w_router, b_router, w_gu, b_gu, w_down, b_down, w_ada_final, b_ada_final, norm_f_g):
    yp, ys = x_prompt, x_sample
    kps, vps, kss, vss, sps, sss, cps, css = [], [], [], [], [], [], [], []
    for layer in range(DEPTH):
        params = (w_ada[layer], b_ada[layer], norm1_g[layer], norm2_g[layer], w_in[layer], sb_bias[layer],
                  conv_w[layer], a_log[layer], dt_bias[layer], gdn_norm_g[layer], w_out[layer], w_router[layer],
                  b_router[layer], w_gu[layer], b_gu[layer], w_down[layer], b_down[layer])
        nb = yp.shape[0]
        conv0 = jnp.zeros((nb, CONV_W - 1, GDN_QKV), yp.dtype)
        s0 = jnp.zeros((nb, N_GDN_HEADS, HEAD_DIM, HEAD_DIM), F32)
        yp, kp, vp, sp, cp = layer_forward(yp, c_prompt, sba_prompt, conv0, s0, *params)
        sba_s = functools.partial(sba_paged, k_pool=cache_k[layer], v_pool=cache_v[layer], page_table=page_table)
        ys, k_s, v_s, s_s, c_s = layer_forward(ys, c_sample, sba_s, state_conv[layer], state_gdn[layer], *params)
        kps.append(kp); vps.append(vp); sps.append(sp); cps.append(cp)
        kss.append(k_s); vss.append(v_s); sss.append(s_s); css.append(c_s)
    y_prompt = final_norm(yp, c_prompt, w_ada_final, b_ada_final, norm_f_g)
    y_sample = final_norm(ys, c_sample, w_ada_final, b_ada_final, norm_f_g)
    return (y_prompt, y_sample, jnp.stack(kps), jnp.stack(vps), jnp.stack(kss), jnp.stack(vss),
            jnp.stack(sps), jnp.stack(sss), jnp.stack(cps), jnp.stack(css))
```

```python
import functools

import jax
import jax.numpy as jnp
from jax import lax
from jax.experimental import pallas as pl
from jax.experimental.pallas import tpu as pltpu

F32 = jnp.float32
BF16 = jnp.bfloat16

LANES = 128
EPS = 1e-6
TOP_K = 4
SWIGLU_LIMIT = 7.0
SWIGLU_ALPHA = 1.702
MOE_BLOCK = 128
GDN_CHUNK = 64
SB_BLOCK = 128
VMEM_LIMIT = 52 * 1024 * 1024
HIGHEST = lax.Precision.HIGHEST


def _cparams(sem):
    return pltpu.CompilerParams(dimension_semantics=sem, vmem_limit_bytes=VMEM_LIMIT)


def _silu(x):
    return x * jax.nn.sigmoid(x)


def _softplus(x):
    return jnp.maximum(x, 0.0) + jnp.log1p(jnp.exp(-jnp.abs(x)))


def _dot(a, b, precision=None):
    return jnp.dot(a, b, preferred_element_type=F32, precision=precision)


def _dot_nt(a, b, precision=None):
    return lax.dot_general(a, b, (((1,), (1,)), ((), ())), preferred_element_type=F32, precision=precision)


def _dot_tn(a, b, precision=None):
    return lax.dot_general(a, b, (((0,), (0,)), ((), ())), preferred_element_type=F32, precision=precision)


def _pick_tile(n, target):
    t = min(n, target)
    while n % t:
        t //= 2
    return t


def _ada_kernel(c_ref, w_ref, b_ref, o_ref):
    a = _silu(c_ref[...]).astype(BF16)
    o_ref[...] = _dot(a, w_ref[...].astype(BF16)) + b_ref[...]


def _ada(c_all, w, b):
    r, d = c_all.shape
    n = w.shape[1]
    tn = _pick_tile(n, 1024)
    return pl.pallas_call(
        _ada_kernel,
        grid=(n // tn,),
        in_specs=[pl.BlockSpec((r, d), lambda j: (0, 0)),
                  pl.BlockSpec((d, tn), lambda j: (0, j)),
                  pl.BlockSpec((1, tn), lambda j: (0, j))],
        out_specs=pl.BlockSpec((r, tn), lambda j: (0, j)),
        out_shape=jax.ShapeDtypeStruct((r, n), F32),
        compiler_params=_cparams(("arbitrary",)),
        name="ada_mod",
    )(c_all, w, b.reshape(1, n))


def _inproj_kernel(x_ref, g_ref, sh_ref, sc_ref, w_ref, wab_ref, proj_ref, gab_ref, h_scr):
    @pl.when(pl.program_id(1) == 0)
    def _():
        x = x_ref[...]
        y = x * lax.rsqrt(jnp.mean(x * x, axis=-1, keepdims=True) + EPS) * g_ref[...]
        h = (y * (1.0 + sc_ref[...]) + sh_ref[...]).astype(BF16)
        h_scr[...] = h
        gab_ref[...] = _dot(h, wab_ref[...])

    proj_ref[0] = _dot(h_scr[...], w_ref[...])


def _mod_spec(arr, tm):
    d = arr.shape[1]
    if arr.shape[0] == 1:
        return pl.BlockSpec((1, d), lambda i, *_: (0, 0))
    return pl.BlockSpec((tm, d), lambda i, *_: (i, 0))


def _inproj(x, g, sh, sc, w_main, w_ab, width):
    t, d = x.shape
    planes = w_main.shape[1] // width
    tm = _pick_tile(t, 512)
    return pl.pallas_call(
        _inproj_kernel,
        grid=(t // tm, planes),
        in_specs=[pl.BlockSpec((tm, d), lambda i, j: (i, 0)),
                  pl.BlockSpec((1, d), lambda i, j: (0, 0)),
                  _mod_spec(sh, tm), _mod_spec(sc, tm),
                  pl.BlockSpec((d, width), lambda i, j: (0, j)),
                  pl.BlockSpec((d, LANES), lambda i, j: (0, 0))],
        out_specs=[pl.BlockSpec((1, tm, width), lambda i, j: (j, i, 0)),
                   pl.BlockSpec((tm, LANES), lambda i, j: (i, 0))],
        out_shape=[jax.ShapeDtypeStruct((planes, t, width), F32),
                   jax.ShapeDtypeStruct((t, LANES), F32)],
        scratch_shapes=[pltpu.VMEM((tm, d), BF16)],
        compiler_params=_cparams(("arbitrary", "arbitrary")),
        name="in_proj",
    )(x, g, sh, sc, w_main, w_ab)


def _suffix_matrix(blk):
    j = lax.broadcasted_iota(jnp.int32, (blk, 2 * blk), 0)
    s = lax.broadcasted_iota(jnp.int32, (blk, 2 * blk), 1)
    return jnp.where((j > s) | (s >= blk), 1.0, 0.0).astype(BF16)


def _sb_block(z, v, valid, acc, lsum, suffix_mat, blk):
    l = jnp.log1p(jnp.exp(-jnp.abs(z)))
    log_1mb = -(jnp.maximum(z, 0.0) + l)
    if valid is not None:
        log_1mb = jnp.where(valid, log_1mb, 0.0)
    hi = log_1mb.astype(BF16)
    lo = (log_1mb - hi.astype(F32)).astype(BF16)
    su = _dot(hi, suffix_mat) + _dot(lo, suffix_mat)
    log_a = jnp.minimum(z, 0.0) - l + su[:, :blk] + lsum
    p = jnp.exp(log_a)
    if valid is not None:
        p = jnp.where(valid, p, 0.0)
    acc = acc + _dot(p.astype(BF16), v)
    return acc, lsum + su[:, blk:]


def _sba_prompt_kernel(bias_ref, q_ref, k_ref, v_ref, o_ref, *, scale, blk):
    h = pl.program_id(0)
    qi = pl.program_id(1)
    q = q_ref[0].astype(BF16)
    bias = bias_ref[h]
    suffix_mat = _suffix_matrix(blk)
    row = lax.broadcasted_iota(jnp.int32, (blk, blk), 0)
    col = lax.broadcasted_iota(jnp.int32, (blk, blk), 1)

    def contrib(kb, acc, lsum, valid):
        start = pl.multiple_of(kb * blk, blk)
        k = k_ref[0, pl.ds(start, blk), :].astype(BF16)
        v = v_ref[0, pl.ds(start, blk), :].astype(BF16)
        z = _dot_nt(q, k) * scale + bias
        return _sb_block(z, v, valid, acc, lsum, suffix_mat, blk)

    zeros = jnp.zeros((blk, blk), F32)
    acc, lsum = contrib(qi, zeros, zeros, col < row)
    acc, _ = lax.fori_loop(0, qi, lambda t, c: contrib(qi - 1 - t, c[0], c[1], None), (acc, lsum))
    o_ref[...] = acc


def _sba_prompt(proj, bias, n_heads):
    _, l, w = proj.shape
    blk = SB_BLOCK
    kern = functools.partial(_sba_prompt_kernel, scale=LANES ** -0.5, blk=blk)
    return pl.pallas_call(
        kern,
        grid=(n_heads, l // blk),
        in_specs=[pl.BlockSpec(memory_space=pltpu.SMEM),
                  pl.BlockSpec((1, blk, LANES), lambda h, i: (0, i, h)),
                  pl.BlockSpec((1, l, LANES), lambda h, i: (1, 0, h)),
                  pl.BlockSpec((1, l, LANES), lambda h, i: (2, 0, h))],
        out_specs=pl.BlockSpec((blk, LANES), lambda h, i: (i, h)),
        out_shape=jax.ShapeDtypeStruct((l, w), F32),
        compiler_params=_cparams(("arbitrary", "arbitrary")),
        name="sba_prompt",
    )(bias, proj, proj, proj)


def _sba_paged_kernel(pt_ref, bias_ref, q_ref, kn_ref, vn_ref, kp_ref, vp_ref, o_ref,
                      wq_scr, bias_scr, acc_scr, lsum_scr, *, n_heads, s, scale):
    p = pl.program_id(1)
    n_pages = pl.num_programs(1)
    page = kp_ref.shape[1]
    hq = n_heads * s
    w = n_heads * LANES
    suffix_mat = _suffix_matrix(page)
    row = lax.broadcasted_iota(jnp.int32, (hq, page), 0)
    col = lax.broadcasted_iota(jnp.int32, (hq, page), 1)

    def contrib(k, v, valid):
        z = _dot_nt(wq_scr[...], k) * scale + bias_scr[...]
        acc, lsum = _sb_block(z, v, valid, acc_scr[...], lsum_scr[...], suffix_mat, page)
        acc_scr[...] = acc
        lsum_scr[...] = lsum

    @pl.when(p == 0)
    def _():
        q = q_ref[0]
        qt = jnp.concatenate([q] * n_heads, axis=0)
        r2 = lax.broadcasted_iota(jnp.int32, (hq, w), 0)
        c2 = lax.broadcasted_iota(jnp.int32, (hq, w), 1)
        wq_scr[...] = jnp.where(r2 // s == c2 // LANES, qt, 0.0).astype(BF16)
        bias = jnp.zeros((hq, page), F32)
        for h in range(n_heads):
            bias = jnp.where(row // s == h, bias_ref[h], bias)
        bias_scr[...] = bias
        acc_scr[...] = jnp.zeros_like(acc_scr)
        lsum_scr[...] = jnp.zeros_like(lsum_scr)
        pad = jnp.zeros((page - s, w), F32)
        kn = jnp.concatenate([kn_ref[0], pad], axis=0).astype(BF16)
        vn = jnp.concatenate([vn_ref[0], pad], axis=0).astype(BF16)
        contrib(kn, vn, col < row % s)

    contrib(kp_ref[0].astype(BF16), vp_ref[0].astype(BF16), None)

    @pl.when(p == n_pages - 1)
    def _():
        for h in range(n_heads):
            o_ref[:, h * LANES:(h + 1) * LANES] = acc_scr[h * s:(h + 1) * s, h * LANES:(h + 1) * LANES]


def _sba_paged(proj, bias, k_pool, v_pool, page_table, n_heads, s):
    _, t, w = proj.shape
    nb, n_pages = page_table.shape
    page = k_pool.shape[1]
    hq = n_heads * s
    kern = functools.partial(_sba_paged_kernel, n_heads=n_heads, s=s, scale=LANES ** -0.5)
    pool_spec = pl.BlockSpec((1, page, w), lambda b, p, pt: (pt[b, n_pages - 1 - p], 0, 0))
    return pl.pallas_call(
        kern,
        grid_spec=pltpu.PrefetchScalarGridSpec(
            num_scalar_prefetch=1,
            grid=(nb, n_pages),
            in_specs=[pl.BlockSpec(memory_space=pltpu.SMEM),
                      pl.BlockSpec((1, s, w), lambda b, p, pt: (0, b, 0)),
                      pl.BlockSpec((1, s, w), lambda b, p, pt: (1, b, 0)),
                      pl.BlockSpec((1, s, w), lambda b, p, pt: (2, b, 0)),
                      pool_spec, pool_spec],
            out_specs=pl.BlockSpec((s, w), lambda b, p, pt: (b, 0)),
            scratch_shapes=[pltpu.VMEM((hq, w), BF16), pltpu.VMEM((hq, page), F32),
                            pltpu.VMEM((hq, w), F32), pltpu.VMEM((hq, page), F32)]),
        out_shape=jax.ShapeDtypeStruct((t, w), F32),
        compiler_params=_cparams(("arbitrary", "arbitrary")),
        name="sba_paged",
    )(page_table, bias, proj, proj, proj, k_pool, v_pool)


def _unit_lower_inverse(lmat, c):
    eye = (lax.broadcasted_iota(jnp.int32, (c, c), 0) == lax.broadcasted_iota(jnp.int32, (c, c), 1)).astype(F32)
    n = -lmat
    inv = eye + n
    k = 2
    while k < c:
        n = _dot(n, n, HIGHEST)
        inv = _dot(inv, eye + n, HIGHEST)
        k *= 2
    return inv


def _gdn_kernel(alog_ref, dtb_ref, q_ref, k_ref, v_ref, z_ref, gab_ref, wq_ref, wk_ref, wv_ref, c0_ref, s0_ref,
                ng_ref, o_ref, s_ref, tail_scr, state_scr, *, c, n_heads, conv_w):
    h = pl.program_id(1)
    ci = pl.program_id(2)
    halo = 8

    @pl.when(ci == 0)
    def _():
        state_scr[...] = s0_ref[0, 0]
        for a in range(3):
            tail_scr[a, 0:halo, :] = jnp.zeros((halo, LANES), F32)
            tail_scr[a, halo - (conv_w - 1):halo, :] = c0_ref[0, a]

    def conv(a, x_ref, w_ref):
        tail_scr[a, halo:halo + c, :] = x_ref[0]
        lo = halo - (conv_w - 1)
        y = sum(tail_scr[a, lo + i:lo + i + c, :] * w_ref[i:i + 1, :] for i in range(conv_w))
        tail_scr[a, 0:halo, :] = tail_scr[a, c:c + halo, :]
        return _silu(y)

    q = conv(0, q_ref, wq_ref)
    k = conv(1, k_ref, wk_ref)
    v = conv(2, v_ref, wv_ref)
    q = q * lax.rsqrt(jnp.sum(q * q, axis=-1, keepdims=True) + EPS) * (LANES ** -0.5)
    k = k * lax.rsqrt(jnp.sum(k * k, axis=-1, keepdims=True) + EPS)

    gab = gab_ref[...]
    lane = lax.broadcasted_iota(jnp.int32, gab.shape, 1)
    ga_col = jnp.sum(jnp.where(lane == h, gab, 0.0), axis=-1, keepdims=True)
    gb_col = jnp.sum(jnp.where(lane == h + n_heads, gab, 0.0), axis=-1, keepdims=True)
    g_col = -jnp.exp(alog_ref[h]) * _softplus(ga_col + dtb_ref[h])
    beta = jax.nn.sigmoid(gb_col)

    ri = lax.broadcasted_iota(jnp.int32, (c, c), 0)
    cj = lax.broadcasted_iota(jnp.int32, (c, c), 1)
    g_row = jnp.sum(jnp.where(ri == cj, g_col, 0.0), axis=0, keepdims=True)
    incl = (cj <= ri).astype(F32)
    gc_col = _dot(incl, jnp.broadcast_to(g_col, (c, LANES)), HIGHEST)
    gc_row = _dot(jnp.broadcast_to(g_row, (8, c)), (ri <= cj).astype(F32), HIGHEST)[0:1]
    decay = jnp.where(cj <= ri, jnp.exp(gc_col[:, 0:1] - gc_row), 0.0)

    kb = k * beta
    lmat = jnp.where(cj < ri, _dot_nt(kb, k, HIGHEST) * decay, 0.0)
    inv = _unit_lower_inverse(lmat, c)
    u = _dot(inv, v * beta, HIGHEST)
    wm = _dot(inv, kb * jnp.exp(gc_col), HIGHEST)
    qk = _dot_nt(q, k, HIGHEST) * decay

    state = state_scr[...]
    v_new = u - _dot(wm, state, HIGHEST)
    o = _dot(q * jnp.exp(gc_col), state, HIGHEST) + _dot(qk, v_new, HIGHEST)
    g_last = gc_col[c - 1:c, :]
    k_dec = k * jnp.exp(g_last - gc_col)
    state = state * jnp.exp(g_last) + _dot_tn(k_dec, v_new, HIGHEST)
    state_scr[...] = state

    o = o * lax.rsqrt(jnp.mean(o * o, axis=-1, keepdims=True) + EPS) * ng_ref[...]
    o_ref[...] = o * _silu(z_ref[0])

    @pl.when(ci == pl.num_programs(2) - 1)
    def _():
        s_ref[0, 0] = state


def _gdn(proj, gab, conv_wt, conv0, s0, a_log, dt_bias, norm_g, n_seq, c):
    _, t, w = proj.shape
    n_heads = w // LANES
    l = t // n_seq
    nc = l // c
    conv_w = conv_wt.shape[0]
    kern = functools.partial(_gdn_kernel, c=c, n_heads=n_heads, conv_w=conv_w)

    def tok(plane):
        return pl.BlockSpec((1, c, LANES), lambda n, h, ci, *_: (plane, n * nc + ci, h))

    def cw(a):
        return pl.BlockSpec((conv_w, LANES), lambda n, h, ci, *_: (0, a * n_heads + h))

    smem = pl.BlockSpec(memory_space=pltpu.SMEM)
    return pl.pallas_call(
        kern,
        grid_spec=pltpu.PrefetchScalarGridSpec(
            num_scalar_prefetch=0,
            grid=(n_seq, n_heads, nc),
            in_specs=[smem, smem, tok(3), tok(4), tok(5), tok(6),
                      pl.BlockSpec((c, LANES), lambda n, h, ci, *_: (n * nc + ci, 0)),
                      cw(0), cw(1), cw(2),
                      pl.BlockSpec((1, 3, conv_w - 1, LANES), lambda n, h, ci, *_: (n, 0, 0, h)),
                      pl.BlockSpec((1, 1, LANES, LANES), lambda n, h, ci, *_: (n, h, 0, 0)),
                      pl.BlockSpec((1, LANES), lambda n, h, ci, *_: (0, 0))],
            out_specs=[pl.BlockSpec((c, LANES), lambda n, h, ci, *_: (n * nc + ci, h)),
                       pl.BlockSpec((1, 1, LANES, LANES), lambda n, h, ci, *_: (n, h, 0, 0))],
            scratch_shapes=[pltpu.VMEM((3, c + 8, LANES), F32), pltpu.VMEM((LANES, LANES), F32)]),
        out_shape=[jax.ShapeDtypeStruct((t, w), F32),
                   jax.ShapeDtypeStruct((n_seq, n_heads, LANES, LANES), F32)],
        compiler_params=_cparams(("arbitrary", "arbitrary", "arbitrary")),
        name="gdn",
    )(a_log, dt_bias, proj, proj, proj, proj, gab, conv_wt, conv_wt, conv_wt, conv0, s0, norm_g)


def _outproj_kernel(osb_ref, ogdn_ref, x_ref, wa_ref, wb_ref, gt_ref, g_ref, sh_ref, sc_ref, wr_ref, br_ref,
                    x1_ref, h_ref, lg_ref):
    mix = _dot(osb_ref[...].astype(BF16), wa_ref[...]) + _dot(ogdn_ref[...].astype(BF16), wb_ref[...])
    x1 = x_ref[...] + gt_ref[...] * mix
    x1_ref[...] = x1
    y = x1 * lax.rsqrt(jnp.mean(x1 * x1, axis=-1, keepdims=True) + EPS) * g_ref[...]
    hf = y * (1.0 + sc_ref[...]) + sh_ref[...]
    h_ref[...] = hf.astype(BF16)
    lg_ref[...] = _dot(hf, wr_ref[...], HIGHEST) + br_ref[...]


def _outproj(o_sb, o_gdn, x, w_a, w_b, gt, g, sh, sc, w_r, b_r):
    t, d = x.shape
    w = o_sb.shape[1]
    tm = _pick_tile(t, 256)
    row = lambda i: (i, 0)
    fixed = lambda i: (0, 0)
    return pl.pallas_call(
        _outproj_kernel,
        grid=(t // tm,),
        in_specs=[pl.BlockSpec((tm, w), row), pl.BlockSpec((tm, w), row), pl.BlockSpec((tm, d), row),
                  pl.BlockSpec((w, d), fixed), pl.BlockSpec((w, d), fixed),
                  _mod_spec(gt, tm), pl.BlockSpec((1, d), fixed), _mod_spec(sh, tm), _mod_spec(sc, tm),
                  pl.BlockSpec((d, LANES), fixed), pl.BlockSpec((1, LANES), fixed)],
        out_specs=[pl.BlockSpec((tm, d), row), pl.BlockSpec((tm, d), row), pl.BlockSpec((tm, LANES), row)],
        out_shape=[jax.ShapeDtypeStruct((t, d), F32), jax.ShapeDtypeStruct((t, d), BF16),
                   jax.ShapeDtypeStruct((t, LANES), F32)],
        compiler_params=_cparams(("arbitrary",)),
        name="out_proj",
    )(o_sb, o_gdn, x, w_a, w_b, gt, g, sh, sc, w_r, b_r)


def _moe_up_kernel(be_ref, na_ref, x_ref, wg_ref, wu_ref, bg_ref, bu_ref, h_ref, wg_scr, wu_scr):
    i = pl.program_id(1)
    fresh = jnp.logical_or(i == 0, be_ref[i] != be_ref[jnp.maximum(i - 1, 0)])

    @pl.when(jnp.logical_and(fresh, i < na_ref[0]))
    def _():
        wg_scr[...] = wg_ref[0].astype(BF16)
        wu_scr[...] = wu_ref[0].astype(BF16)

    @pl.when(i < na_ref[0])
    def _():
        x = x_ref[...]
        gate = jnp.minimum(_dot(x, wg_scr[...]) + bg_ref[0], SWIGLU_LIMIT)
        up = jnp.clip(_dot(x, wu_scr[...]) + bu_ref[0], -SWIGLU_LIMIT, SWIGLU_LIMIT)
        h_ref[...] = ((up + 1.0) * gate * jax.nn.sigmoid(gate * SWIGLU_ALPHA)).astype(BF16)


def _moe_down_kernel(be_ref, na_ref, h_ref, wd_ref, bd_ref, y_ref, wd_scr):
    i = pl.program_id(1)
    fresh = jnp.logical_or(i == 0, be_ref[i] != be_ref[jnp.maximum(i - 1, 0)])

    @pl.when(jnp.logical_and(fresh, i < na_ref[0]))
    def _():
        wd_scr[...] = wd_ref[0].astype(BF16)

    @pl.when(i < na_ref[0])
    def _():
        y_ref[...] = _dot(h_ref[...], wd_scr[...]) + bd_ref[0]


def _moe_experts(xb, block_e, n_active, w_gu, b_gu, w_down, b_down):
    cap, d = xb.shape
    n_blocks = cap // MOE_BLOCK
    n_exp, _, ff2 = w_gu.shape
    ff = ff2 // 2
    tf = _pick_tile(ff, 512)
    nf = ff // tf

    def blk(i, na):
        return jnp.minimum(i, na[0] - 1)

    h = pl.pallas_call(
        _moe_up_kernel,
        grid_spec=pltpu.PrefetchScalarGridSpec(
            num_scalar_prefetch=2,
            grid=(nf, n_blocks),
            in_specs=[pl.BlockSpec((MOE_BLOCK, d), lambda j, i, be, na: (blk(i, na), 0)),
                      pl.BlockSpec((1, d, tf), lambda j, i, be, na: (be[blk(i, na)], 0, j)),
                      pl.BlockSpec((1, d, tf), lambda j, i, be, na: (be[blk(i, na)], 0, nf + j)),
                      pl.BlockSpec((1, 1, tf), lambda j, i, be, na: (be[blk(i, na)], 0, j)),
                      pl.BlockSpec((1, 1, tf), lambda j, i, be, na: (be[blk(i, na)], 0, nf + j))],
            out_specs=pl.BlockSpec((MOE_BLOCK, tf), lambda j, i, be, na: (blk(i, na), j)),
            scratch_shapes=[pltpu.VMEM((d, tf), BF16), pltpu.VMEM((d, tf), BF16)]),
        out_shape=jax.ShapeDtypeStruct((cap, ff), BF16),
        compiler_params=_cparams(("arbitrary", "arbitrary")),
        name="moe_up",
    )(block_e, n_active, xb, w_gu, w_gu, b_gu.reshape(n_exp, 1, ff2), b_gu.reshape(n_exp, 1, ff2))

    tn = _pick_tile(d, 1024)
    return pl.pallas_call(
        _moe_down_kernel,
        grid_spec=pltpu.PrefetchScalarGridSpec(
            num_scalar_prefetch=2,
            grid=(d // tn, n_blocks),
            in_specs=[pl.BlockSpec((MOE_BLOCK, ff), lambda j, i, be, na: (blk(i, na), 0)),
                      pl.BlockSpec((1, ff, tn), lambda j, i, be, na: (be[blk(i, na)], 0, j)),
                      pl.BlockSpec((1, 1, tn), lambda j, i, be, na: (be[blk(i, na)], 0, j))],
            out_specs=pl.BlockSpec((MOE_BLOCK, tn), lambda j, i, be, na: (blk(i, na), j)),
            scratch_shapes=[pltpu.VMEM((ff, tn), BF16)]),
        out_shape=jax.ShapeDtypeStruct((cap, d), F32),
        compiler_params=_cparams(("arbitrary", "arbitrary")),
        name="moe_down",
    )(block_e, n_active, h, w_down, b_down.reshape(n_exp, 1, d))


def _route(logits, n_exp):
    t = logits.shape[0]
    top_logit, top_idx = lax.top_k(logits, TOP_K)
    gates = jax.nn.softmax(top_logit, axis=-1)
    n_slots = t * TOP_K
    flat_e = top_idx.reshape(-1)
    onehot = (flat_e[:, None] == jnp.arange(n_exp)[None, :]).astype(jnp.int32)
    rank = jnp.take_along_axis(jnp.cumsum(onehot, axis=0) - onehot, flat_e[:, None], axis=1)[:, 0]
    counts = jnp.sum(onehot, axis=0)
    padded = (counts + MOE_BLOCK - 1) // MOE_BLOCK * MOE_BLOCK
    pad_end = jnp.cumsum(padded)
    pad_start = pad_end - padded
    dest = (pad_start[flat_e] + rank).astype(jnp.int32)
    n_blocks = -(-n_slots // MOE_BLOCK) + n_exp
    cap = n_blocks * MOE_BLOCK
    src_tok = jnp.zeros((cap,), jnp.int32).at[dest].set(jnp.arange(n_slots, dtype=jnp.int32) // TOP_K)
    block_e = jnp.minimum(jnp.searchsorted(pad_end, jnp.arange(n_blocks) * MOE_BLOCK, side='right'),
                          n_exp - 1).astype(jnp.int32)
    n_active = (pad_end[-1] // MOE_BLOCK).astype(jnp.int32).reshape(1)
    return gates, dest.reshape(t, TOP_K), src_tok, block_e, n_active


def _final_kernel(x_ref, f_ref, gt_ref, g_ref, sh_ref, sc_ref, o_ref):
    x = x_ref[...] + gt_ref[...] * f_ref[...]
    y = x * lax.rsqrt(jnp.mean(x * x, axis=-1, keepdims=True) + EPS) * g_ref[...]
    o_ref[...] = y * (1.0 + sc_ref[...]) + sh_ref[...]


def _final(x1, ffn, gt, g, sh, sc):
    t, d = x1.shape
    tm = _pick_tile(t, 512)
    row = lambda i: (i, 0)
    return pl.pallas_call(
        _final_kernel,
        grid=(t // tm,),
        in_specs=[pl.BlockSpec((tm, d), row), pl.BlockSpec((tm, d), row), _mod_spec(gt, tm),
                  pl.BlockSpec((1, d), lambda i: (0, 0)), _mod_spec(sh, tm), _mod_spec(sc, tm)],
        out_specs=pl.BlockSpec((tm, d), row),
        out_shape=jax.ShapeDtypeStruct((t, d), F32),
        compiler_params=_cparams(("arbitrary",)),
        name="final_norm",
    )(x1, ffn, gt, g, sh, sc)


def kernel(x_prompt, x_sample, cache_k, cache_v, state_gdn, state_conv, page_table, c_prompt, c_sample, w_ada, b_ada, norm1_g, norm2_g, w_in, sb_bias, conv_w, a_log, dt_bias, gdn_norm_g, w_out, w_router, b_router, w_gu, b_gu, w_down, b_down, w_ada_final, b_ada_final, norm_f_g):
    depth = w_ada.shape[0]
    assert depth == 1, "single-layer stack"
    nbp, seq, d = x_prompt.shape
    nbs, dec_seq, _ = x_sample.shape
    assert nbp == 1
    n_sb = cache_k.shape[-2]
    n_gdn = state_gdn.shape[2]
    width = n_sb * LANES
    assert cache_k.shape[-1] == LANES and n_gdn * LANES == width
    conv_taps = conv_w.shape[-1]
    n_exp = w_router.shape[-1]
    tp, ts = nbp * seq, nbs * dec_seq

    c_all = jnp.concatenate([c_prompt, c_sample], axis=0)
    n_c = c_all.shape[0]
    c_all = jnp.pad(c_all, ((0, -n_c % 8), (0, 0)))
    mods = _ada(c_all, w_ada[0], b_ada[0])[:n_c]
    fin = _ada(c_all, w_ada_final, b_ada_final)[:n_c]
    mods_p = [m for m in jnp.split(mods[:nbp], 6, axis=-1)]
    mods_s = [jnp.repeat(m, dec_seq, axis=0) for m in jnp.split(mods[nbp:], 6, axis=-1)]
    fin_p = jnp.split(fin[:nbp], 2, axis=-1)
    fin_s = [jnp.repeat(m, dec_seq, axis=0) for m in jnp.split(fin[nbp:], 2, axis=-1)]

    w_main = w_in[0][:, :7 * width].astype(BF16)
    w_ab = jnp.pad(w_in[0][:, 7 * width:], ((0, 0), (0, LANES - 2 * n_gdn))).astype(BF16)
    g1 = norm1_g[0].reshape(1, d)
    g2 = norm2_g[0].reshape(1, d)
    conv_wt = conv_w[0].T
    w_oa = w_out[0][:width].astype(BF16)
    w_ob = w_out[0][width:].astype(BF16)
    w_r = jnp.pad(w_router[0], ((0, 0), (0, LANES - n_exp)))
    b_r = jnp.pad(b_router[0], (0, LANES - n_exp), constant_values=-1e30).reshape(1, LANES)
    ng = gdn_norm_g[0].reshape(1, LANES)

    def mixer(x, m, sba_fn, conv0, s0, n_seq, chunk):
        sh1, sc1, gt1, sh2, sc2, _ = m
        proj, gab = _inproj(x, g1, sh1, sc1, w_main, w_ab, width)
        o_sb = sba_fn(proj)
        o_gdn, s_new = _gdn(proj, gab, conv_wt, conv0, s0, a_log[0], dt_bias[0], ng, n_seq, chunk)
        x1, hffn, logits = _outproj(o_sb, o_gdn, x, w_oa, w_ob, gt1, g2, sh2, sc2, w_r, b_r)
        return proj, s_new, x1, hffn, logits[:, :n_exp]

    def conv_layout(c0):
        n = c0.shape[0]
        return c0.reshape(n, conv_taps - 1, 3, width).transpose(0, 2, 1, 3)

    xp = x_prompt.reshape(tp, d)
    xs = x_sample.reshape(ts, d)
    conv0_p = jnp.zeros((nbp, 3, conv_taps - 1, width), F32)
    s0_p = jnp.zeros((nbp, n_gdn, LANES, LANES), F32)
    proj_p, s_p, x1_p, h_p, lg_p = mixer(
        xp, mods_p, lambda pr: _sba_prompt(pr, sb_bias[0], n_sb), conv0_p, s0_p, nbp, GDN_CHUNK)
    k_pool = cache_k[0].reshape(cache_k.shape[1], cache_k.shape[2], width)
    v_pool = cache_v[0].reshape(cache_v.shape[1], cache_v.shape[2], width)
    proj_s, s_s, x1_s, h_s, lg_s = mixer(
        xs, mods_s, lambda pr: _sba_paged(pr, sb_bias[0], k_pool, v_pool, page_table, n_sb, dec_seq),
        conv_layout(state_conv[0]), state_gdn[0], nbs, dec_seq)

    hffn = jnp.concatenate([h_p, h_s], axis=0)
    logits = jnp.concatenate([lg_p, lg_s], axis=0)
    gates, dest, src_tok, block_e, n_active = _route(logits, n_exp)
    xb = hffn[src_tok]
    y = _moe_experts(xb, block_e, n_active, w_gu[0], b_gu[0], w_down[0], b_down[0])
    ffn = jnp.sum(y[dest] * gates[..., None], axis=1)

    y_p = _final(x1_p, ffn[:tp], mods_p[5], norm_f_g.reshape(1, d), fin_p[0], fin_p[1])
    y_s = _final(x1_s, ffn[tp:], mods_s[5], norm_f_g.reshape(1, d), fin_s[0], fin_s[1])

    def new_conv(proj, n_seq):
        t = proj.shape[1]
        tail = proj[3:6].reshape(3, n_seq, t // n_seq, width)[:, :, -(conv_taps - 1):]
        return tail.transpose(1, 2, 0, 3).reshape(n_seq, conv_taps - 1, 3 * width)

    heads = lambda a, n, l: a.reshape(1, n, l, n_sb, LANES)
    return (y_p.reshape(nbp, seq, d), y_s.reshape(nbs, dec_seq, d),
            heads(proj_p[1], nbp, seq), heads(proj_p[2], nbp, seq),
            heads(proj_s[1], nbs, dec_seq), heads(proj_s[2], nbs, dec_seq),
            s_p[None], s_s[None], new_conv(proj_p, nbp)[None], new_conv(proj_s, nbs)[None])
```

```python
import functools

import jax
import jax.numpy as jnp
from jax import lax
from jax.experimental import pallas as pl
from jax.experimental.pallas import tpu as pltpu

F32 = jnp.float32
BF16 = jnp.bfloat16

LANES = 128
EPS = 1e-6
TOP_K = 4
SWIGLU_LIMIT = 7.0
SWIGLU_ALPHA = 1.702
MOE_BLOCK = 128
GDN_CHUNK = 64
SB_BLOCK = 256
SB_TILE_BLOCKS = 4
VMEM_LIMIT = 52 * 1024 * 1024
HIGHEST = lax.Precision.HIGHEST
LOG2E = 1.4426950408889634


def _cparams(sem):
    return pltpu.CompilerParams(dimension_semantics=sem, vmem_limit_bytes=VMEM_LIMIT)


def _silu(x):
    return x * jax.nn.sigmoid(x)


def _softplus(x):
    return jnp.maximum(x, 0.0) + jnp.log1p(jnp.exp(-jnp.abs(x)))


def _dot(a, b, precision=None):
    return jnp.dot(a, b, preferred_element_type=F32, precision=precision)


def _dot_nt(a, b, precision=None):
    return lax.dot_general(a, b, (((1,), (1,)), ((), ())), preferred_element_type=F32, precision=precision)


def _dot_tn(a, b, precision=None):
    return lax.dot_general(a, b, (((0,), (0,)), ((), ())), preferred_element_type=F32, precision=precision)


def _pick_tile(n, target):
    t = min(n, target)
    while n % t:
        t //= 2
    return t


def _ada_kernel(c_ref, w_ref, b_ref, o_ref):
    a = _silu(c_ref[...]).astype(BF16)
    o_ref[...] = _dot(a, w_ref[...].astype(BF16)) + b_ref[...]


def _ada(c_all, w, b):
    r, d = c_all.shape
    n = w.shape[1]
    tn = _pick_tile(n, 1024)
    return pl.pallas_call(
        _ada_kernel,
        grid=(n // tn,),
        in_specs=[pl.BlockSpec((r, d), lambda j: (0, 0)),
                  pl.BlockSpec((d, tn), lambda j: (0, j)),
                  pl.BlockSpec((1, tn), lambda j: (0, j))],
        out_specs=pl.BlockSpec((r, tn), lambda j: (0, j)),
        out_shape=jax.ShapeDtypeStruct((r, n), F32),
        compiler_params=_cparams(("arbitrary",)),
        name="ada_mod",
    )(c_all, w, b.reshape(1, n))


def _inproj_kernel(x_ref, g_ref, sh_ref, sc_ref, w_ref, wab_ref, proj_ref, gab_ref, h_scr):
    @pl.when(pl.program_id(1) == 0)
    def _():
        x = x_ref[...]
        y = x * lax.rsqrt(jnp.mean(x * x, axis=-1, keepdims=True) + EPS) * g_ref[...]
        h = (y * (1.0 + sc_ref[...]) + sh_ref[...]).astype(BF16)
        h_scr[...] = h
        gab_ref[...] = _dot(h, wab_ref[...])

    proj_ref[0] = _dot(h_scr[...], w_ref[...])


def _mod_spec(arr, tm):
    d = arr.shape[1]
    if arr.shape[0] == 1:
        return pl.BlockSpec((1, d), lambda i, *_: (0, 0))
    return pl.BlockSpec((tm, d), lambda i, *_: (i, 0))


def _inproj(x, g, sh, sc, w_main, w_ab, width):
    t, d = x.shape
    planes = w_main.shape[1] // width
    tm = _pick_tile(t, 512)
    return pl.pallas_call(
        _inproj_kernel,
        grid=(t // tm, planes),
        in_specs=[pl.BlockSpec((tm, d), lambda i, j: (i, 0)),
                  pl.BlockSpec((1, d), lambda i, j: (0, 0)),
                  _mod_spec(sh, tm), _mod_spec(sc, tm),
                  pl.BlockSpec((d, width), lambda i, j: (0, j)),
                  pl.BlockSpec((d, LANES), lambda i, j: (0, 0))],
        out_specs=[pl.BlockSpec((1, tm, width), lambda i, j: (j, i, 0)),
                   pl.BlockSpec((tm, LANES), lambda i, j: (i, 0))],
        out_shape=[jax.ShapeDtypeStruct((planes, t, width), F32),
                   jax.ShapeDtypeStruct((t, LANES), F32)],
        scratch_shapes=[pltpu.VMEM((tm, d), BF16)],
        compiler_params=_cparams(("arbitrary", "arbitrary")),
        name="in_proj",
    )(x, g, sh, sc, w_main, w_ab)


def _cum_matrix(blk):
    j = lax.broadcasted_iota(jnp.int32, (blk, blk + LANES), 0)
    s = lax.broadcasted_iota(jnp.int32, (blk, blk + LANES), 1)
    return jnp.where((j >= s) | (s >= blk), 1.0, 0.0).astype(BF16)


def _sb_block(zz, v, valid, cum_mat, blk):
    y = jnp.maximum(zz, 0.0) + jnp.log2(1.0 + jnp.exp2(-jnp.abs(zz)))
    if valid is not None:
        y = jnp.where(valid, y, 0.0)
    su = _dot(y.astype(BF16), cum_mat)
    p = jnp.exp2(zz - su[:, :blk])
    if valid is not None:
        p = jnp.where(valid, p, 0.0)
    return _dot(p.astype(BF16), v), su[:, blk:]


def _sba_prompt_kernel(bias_ref, q_ref, k_ref, v_ref, o_ref, q_scr, k_scr, v_scr, acc_scr, lsum_scr,
                       *, scale2, blk, ns):
    h = pl.program_id(0)
    qi = pl.program_id(1)
    tq = ns * blk

    @pl.when(qi == 0)
    def _():
        k_scr[...] = k_ref[0].astype(BF16)
        v_scr[...] = v_ref[0].astype(BF16)

    q_scr[...] = (q_ref[0] * scale2).astype(BF16)
    acc_scr[...] = jnp.zeros_like(acc_scr)
    lsum_scr[...] = jnp.zeros_like(lsum_scr)
    bias2 = bias_ref[h] * LOG2E
    cum_mat = _cum_matrix(blk)

    def visit(kb, row0, valid):
        start = pl.multiple_of(kb * blk, blk)
        zz = _dot_nt(q_scr[row0:tq, :], k_scr[pl.ds(start, blk), :]) + bias2
        pv, tot = _sb_block(zz, v_scr[pl.ds(start, blk), :], valid, cum_mat, blk)
        lsum = lsum_scr[row0:tq, :]
        acc_scr[row0:tq, :] += jnp.exp2(-lsum) * pv
        lsum_scr[row0:tq, :] = lsum + tot

    for j in reversed(range(ns)):
        row = lax.broadcasted_iota(jnp.int32, (tq - j * blk, blk), 0)
        col = lax.broadcasted_iota(jnp.int32, (tq - j * blk, blk), 1)
        visit(qi * ns + j, j * blk, col < row)

    def earlier(t, carry):
        visit(qi * ns - 1 - t, 0, None)
        return carry

    lax.fori_loop(0, qi * ns, earlier, 0)
    o_ref[...] = acc_scr[...]


def _sba_prompt(proj, bias, n_heads):
    _, l, w = proj.shape
    blk = SB_BLOCK
    ns = _pick_tile(l // blk, SB_TILE_BLOCKS)
    tq = ns * blk
    kern = functools.partial(_sba_prompt_kernel, scale2=LANES ** -0.5 * LOG2E, blk=blk, ns=ns)
    return pl.pallas_call(
        kern,
        grid=(n_heads, l // tq),
        in_specs=[pl.BlockSpec(memory_space=pltpu.SMEM),
                  pl.BlockSpec((1, tq, LANES), lambda h, i: (0, i, h)),
                  pl.BlockSpec((1, l, LANES), lambda h, i: (1, 0, h)),
                  pl.BlockSpec((1, l, LANES), lambda h, i: (2, 0, h))],
        out_specs=pl.BlockSpec((tq, LANES), lambda h, i: (i, h)),
        out_shape=jax.ShapeDtypeStruct((l, w), F32),
        scratch_shapes=[pltpu.VMEM((tq, LANES), BF16), pltpu.VMEM((l, LANES), BF16), pltpu.VMEM((l, LANES), BF16),
                        pltpu.VMEM((tq, LANES), F32), pltpu.VMEM((tq, LANES), F32)],
        compiler_params=_cparams(("arbitrary", "arbitrary")),
        name="sba_prompt",
    )(bias, proj, proj, proj)


def _sba_paged_kernel(pt_ref, bias_ref, q_ref, kn_ref, vn_ref, *rest, n_heads, s, scale2, group):
    kp_refs = rest[:group]
    vp_refs = rest[group:2 * group]
    o_ref, wq_scr, bias_scr, acc_scr, lsum_scr = rest[2 * group:]
    p = pl.program_id(1)
    page = kp_refs[0].shape[1] // n_heads
    hq = n_heads * s
    w = n_heads * LANES
    cum_mat = _cum_matrix(page)

    def visit(ks, vs, valid):
        n = len(ks)
        zz = _dot_nt(wq_scr[...], jnp.concatenate(ks, axis=0))
        zz = jnp.concatenate([zz[:, g * page:(g + 1) * page] for g in range(n)], axis=0) + \
            jnp.concatenate([bias_scr[...]] * n, axis=0)
        y = jnp.maximum(zz, 0.0) + jnp.log2(1.0 + jnp.exp2(-jnp.abs(zz)))
        if valid is not None:
            y = jnp.where(valid, y, 0.0)
        su = _dot(y.astype(BF16), cum_mat)
        p = jnp.exp2(zz - su[:, :page])
        if valid is not None:
            p = jnp.where(valid, p, 0.0)
        p = p.astype(BF16)
        acc = acc_scr[...]
        lsum = lsum_scr[...]
        for g in range(n):
            pv = _dot(p[g * hq:(g + 1) * hq], vs[g])
            pv = jnp.concatenate([pv[h * s:(h + 1) * s, h * LANES:(h + 1) * LANES] for h in range(n_heads)], axis=0)
            acc = acc + jnp.exp2(-lsum) * pv
            lsum = lsum + su[g * hq:(g + 1) * hq, page:]
        acc_scr[...] = acc
        lsum_scr[...] = lsum

    @pl.when(p == 0)
    def _():
        qt = jnp.concatenate([q_ref[0] * scale2] * n_heads, axis=0)
        r2 = lax.broadcasted_iota(jnp.int32, (hq, w), 0)
        c2 = lax.broadcasted_iota(jnp.int32, (hq, w), 1)
        wq_scr[...] = jnp.where(r2 // s == c2 // LANES, qt, 0.0).astype(BF16)
        row = lax.broadcasted_iota(jnp.int32, (hq, page), 0)
        col = lax.broadcasted_iota(jnp.int32, (hq, page), 1)
        bias = jnp.zeros((hq, page), F32)
        for h in range(n_heads):
            bias = jnp.where(row // s == h, bias_ref[h] * LOG2E, bias)
        bias_scr[...] = bias
        acc_scr[...] = jnp.zeros_like(acc_scr)
        lsum_scr[...] = jnp.zeros_like(lsum_scr)
        pad = jnp.zeros((page - s, w), F32)
        kn = jnp.concatenate([kn_ref[0], pad], axis=0).astype(BF16)
        vn = jnp.concatenate([vn_ref[0], pad], axis=0).astype(BF16)
        visit([kn], [vn], col < row % s)

    def head_major(ref):
        return jnp.concatenate([ref[0, pl.ds(h, page, stride=n_heads), :] for h in range(n_heads)],
                               axis=1).astype(BF16)

    visit([head_major(r) for r in kp_refs], [head_major(r) for r in vp_refs], None)

    @pl.when(p == pl.num_programs(1) - 1)
    def _():
        for h in range(n_heads):
            o_ref[:, h * LANES:(h + 1) * LANES] = acc_scr[h * s:(h + 1) * s, :]


def _sba_paged(proj, bias, k_pool, v_pool, page_table, n_heads, s):
    _, t, w = proj.shape
    nb, n_pages = page_table.shape
    rows = k_pool.shape[1]
    page = rows // n_heads
    hq = n_heads * s
    group = _pick_tile(n_pages, 4)
    kern = functools.partial(_sba_paged_kernel, n_heads=n_heads, s=s, scale2=LANES ** -0.5 * LOG2E, group=group)

    def pool_spec(g):
        return pl.BlockSpec((1, rows, LANES), lambda b, p, pt: (pt[b, n_pages - 1 - (p * group + g)], 0, 0))

    pool_specs = [pool_spec(g) for g in range(group)]
    return pl.pallas_call(
        kern,
        grid_spec=pltpu.PrefetchScalarGridSpec(
            num_scalar_prefetch=1,
            grid=(nb, n_pages // group),
            in_specs=[pl.BlockSpec(memory_space=pltpu.SMEM),
                      pl.BlockSpec((1, s, w), lambda b, p, pt: (0, b, 0)),
                      pl.BlockSpec((1, s, w), lambda b, p, pt: (1, b, 0)),
                      pl.BlockSpec((1, s, w), lambda b, p, pt: (2, b, 0))] + pool_specs + pool_specs,
            out_specs=pl.BlockSpec((s, w), lambda b, p, pt: (b, 0)),
            scratch_shapes=[pltpu.VMEM((hq, w), BF16), pltpu.VMEM((hq, page), F32),
                            pltpu.VMEM((hq, LANES), F32), pltpu.VMEM((hq, page), F32)]),
        out_shape=jax.ShapeDtypeStruct((t, w), F32),
        compiler_params=_cparams(("arbitrary", "arbitrary")),
        name="sba_paged",
    )(page_table, bias, proj, proj, proj, *([k_pool] * group), *([v_pool] * group))


def _gdn_chunk(q, k, v, g_col, beta, state, c):
    heads = range(len(q))
    ri = lax.broadcasted_iota(jnp.int32, (c, c), 0)
    cj = lax.broadcasted_iota(jnp.int32, (c, c), 1)
    eye = (ri == cj).astype(F32)
    incl = (cj <= ri).astype(F32)
    incl_t = (ri <= cj).astype(F32)
    g_row = [jnp.sum(jnp.where(ri == cj, g_col[h], 0.0), axis=0, keepdims=True) for h in heads]
    gc_col = [_dot(incl, jnp.broadcast_to(g_col[h], (c, LANES)), HIGHEST) for h in heads]
    gc_row = [_dot(jnp.broadcast_to(g_row[h], (8, c)), incl_t, HIGHEST)[0:1] for h in heads]
    decay = [jnp.where(cj <= ri, jnp.exp(gc_col[h][:, 0:1] - gc_row[h]), 0.0) for h in heads]
    kb = [k[h] * beta[h] for h in heads]
    kk = [_dot_nt(kb[h], k[h], HIGHEST) for h in heads]
    qk = [_dot_nt(q[h], k[h], HIGHEST) * decay[h] for h in heads]
    n = [jnp.where(cj < ri, -kk[h] * decay[h], 0.0) for h in heads]
    inv = [eye + n[h] for h in heads]
    i = 2
    while i < c:
        n = [_dot(n[h], n[h], HIGHEST) for h in heads]
        inv = [_dot(inv[h], eye + n[h], HIGHEST) for h in heads]
        i *= 2
    rhs = [jnp.concatenate([v[h] * beta[h], kb[h] * jnp.exp(gc_col[h])], axis=1) for h in heads]
    sol = [_dot(inv[h], rhs[h], HIGHEST) for h in heads]
    ws = [_dot(sol[h][:, LANES:], state[h], HIGHEST) for h in heads]
    qs = [_dot(q[h] * jnp.exp(gc_col[h]), state[h], HIGHEST) for h in heads]
    v_new = [sol[h][:, :LANES] - ws[h] for h in heads]
    g_last = [gc_col[h][c - 1:c, :] for h in heads]
    k_dec = [k[h] * jnp.exp(g_last[h] - gc_col[h]) for h in heads]
    o = [qs[h] + _dot(qk[h], v_new[h], HIGHEST) for h in heads]
    new_state = [state[h] * jnp.exp(g_last[h]) + _dot_tn(k_dec[h], v_new[h], HIGHEST) for h in heads]
    return o, new_state


def _gdn_kernel(alog_ref, dtb_ref, q_ref, k_ref, v_ref, z_ref, gab_ref, cw_ref, c0_ref, s0_ref, ng_ref,
                o_ref, s_ref, tail_scr, state_scr, *, c, n_heads, conv_w):
    ci = pl.program_id(1)
    halo = 8
    w = n_heads * LANES

    @pl.when(ci == 0)
    def _():
        state_scr[...] = s0_ref[0]
        for a in range(3):
            tail_scr[a, 0:halo, :] = jnp.zeros((halo, w), F32)
            tail_scr[a, halo - (conv_w - 1):halo, :] = c0_ref[0, a]

    def conv(a, x_ref):
        tail_scr[a, halo:halo + c, :] = x_ref[0]
        lo = halo - (conv_w - 1)
        y = sum(tail_scr[a, lo + i:lo + i + c, :] * cw_ref[i:i + 1, a * w:(a + 1) * w] for i in range(conv_w))
        tail_scr[a, 0:halo, :] = tail_scr[a, c:c + halo, :]
        return _silu(y)

    q_all = conv(0, q_ref)
    k_all = conv(1, k_ref)
    v_all = conv(2, v_ref)
    gab = gab_ref[...]
    lane = lax.broadcasted_iota(jnp.int32, gab.shape, 1)

    heads = range(n_heads)
    sl = [slice(h * LANES, (h + 1) * LANES) for h in heads]
    q = [q_all[:, sl[h]] for h in heads]
    k = [k_all[:, sl[h]] for h in heads]
    q = [q[h] * lax.rsqrt(jnp.sum(q[h] * q[h], axis=-1, keepdims=True) + EPS) * (LANES ** -0.5) for h in heads]
    k = [k[h] * lax.rsqrt(jnp.sum(k[h] * k[h], axis=-1, keepdims=True) + EPS) for h in heads]
    ga = [jnp.sum(jnp.where(lane == h, gab, 0.0), axis=-1, keepdims=True) for h in heads]
    gb = [jnp.sum(jnp.where(lane == h + n_heads, gab, 0.0), axis=-1, keepdims=True) for h in heads]
    g_col = [-jnp.exp(alog_ref[h]) * _softplus(ga[h] + dtb_ref[h]) for h in heads]
    beta = [jax.nn.sigmoid(gb[h]) for h in heads]
    o, state = _gdn_chunk(q, k, [v_all[:, sl[h]] for h in heads], g_col, beta, [state_scr[h] for h in heads], c)
    for h in heads:
        state_scr[h] = state[h]
        on = o[h] * lax.rsqrt(jnp.mean(o[h] * o[h], axis=-1, keepdims=True) + EPS) * ng_ref[...]
        o_ref[:, sl[h]] = on * _silu(z_ref[0, :, sl[h]])

    @pl.when(ci == pl.num_programs(1) - 1)
    def _():
        s_ref[0] = state_scr[...]


def _gdn(proj, gab, conv_wt, conv0, s0, a_log, dt_bias, norm_g, n_seq, c):
    _, t, w = proj.shape
    n_heads = w // LANES
    l = t // n_seq
    nc = l // c
    conv_w = conv_wt.shape[0]
    kern = functools.partial(_gdn_kernel, c=c, n_heads=n_heads, conv_w=conv_w)

    def tok(plane):
        return pl.BlockSpec((1, c, w), lambda n, ci: (plane, n * nc + ci, 0))

    smem = pl.BlockSpec(memory_space=pltpu.SMEM)
    return pl.pallas_call(
        kern,
        grid=(n_seq, nc),
        in_specs=[smem, smem, tok(3), tok(4), tok(5), tok(6),
                  pl.BlockSpec((c, LANES), lambda n, ci: (n * nc + ci, 0)),
                  pl.BlockSpec((conv_w, 3 * w), lambda n, ci: (0, 0)),
                  pl.BlockSpec((1, 3, conv_w - 1, w), lambda n, ci: (n, 0, 0, 0)),
                  pl.BlockSpec((1, n_heads, LANES, LANES), lambda n, ci: (n, 0, 0, 0)),
                  pl.BlockSpec((1, LANES), lambda n, ci: (0, 0))],
        out_specs=[pl.BlockSpec((c, w), lambda n, ci: (n * nc + ci, 0)),
                   pl.BlockSpec((1, n_heads, LANES, LANES), lambda n, ci: (n, 0, 0, 0))],
        out_shape=[jax.ShapeDtypeStruct((t, w), F32),
                   jax.ShapeDtypeStruct((n_seq, n_heads, LANES, LANES), F32)],
        scratch_shapes=[pltpu.VMEM((3, c + 8, w), F32), pltpu.VMEM((n_heads, LANES, LANES), F32)],
        compiler_params=_cparams(("arbitrary", "arbitrary")),
        name="gdn",
    )(a_log, dt_bias, proj, proj, proj, proj, gab, conv_wt, conv0, s0, norm_g)


def _outproj_kernel(osb_ref, ogdn_ref, x_ref, wa_ref, wb_ref, gt_ref, g_ref, sh_ref, sc_ref, wr_ref, br_ref,
                    x1_ref, h_ref, lg_ref):
    mix = _dot(osb_ref[...].astype(BF16), wa_ref[...]) + _dot(ogdn_ref[...].astype(BF16), wb_ref[...])
    x1 = x_ref[...] + gt_ref[...] * mix
    x1_ref[...] = x1
    y = x1 * lax.rsqrt(jnp.mean(x1 * x1, axis=-1, keepdims=True) + EPS) * g_ref[...]
    hf = y * (1.0 + sc_ref[...]) + sh_ref[...]
    h_ref[...] = hf
    lg_ref[...] = _dot(hf, wr_ref[...], HIGHEST) + br_ref[...]


def _outproj(o_sb, o_gdn, x, w_a, w_b, gt, g, sh, sc, w_r, b_r):
    t, d = x.shape
    w = o_sb.shape[1]
    tm = _pick_tile(t, 256)
    row = lambda i: (i, 0)
    fixed = lambda i: (0, 0)
    return pl.pallas_call(
        _outproj_kernel,
        grid=(t // tm,),
        in_specs=[pl.BlockSpec((tm, w), row), pl.BlockSpec((tm, w), row), pl.BlockSpec((tm, d), row),
                  pl.BlockSpec((w, d), fixed), pl.BlockSpec((w, d), fixed),
                  _mod_spec(gt, tm), pl.BlockSpec((1, d), fixed), _mod_spec(sh, tm), _mod_spec(sc, tm),
                  pl.BlockSpec((d, LANES), fixed), pl.BlockSpec((1, LANES), fixed)],
        out_specs=[pl.BlockSpec((tm, d), row), pl.BlockSpec((tm, d), row), pl.BlockSpec((tm, LANES), row)],
        out_shape=[jax.ShapeDtypeStruct((t, d), F32), jax.ShapeDtypeStruct((t, d), F32),
                   jax.ShapeDtypeStruct((t, LANES), F32)],
        compiler_params=_cparams(("arbitrary",)),
        name="out_proj",
    )(o_sb, o_gdn, x, w_a, w_b, gt, g, sh, sc, w_r, b_r)


def _moe_up_kernel(be_ref, na_ref, x_ref, wg_ref, wu_ref, bg_ref, bu_ref, h_ref, wg_scr, wu_scr):
    i = pl.program_id(1)
    fresh = jnp.logical_or(i == 0, be_ref[i] != be_ref[jnp.maximum(i - 1, 0)])

    @pl.when(jnp.logical_and(fresh, i < na_ref[0]))
    def _():
        wg_scr[...] = wg_ref[0].astype(BF16)
        wu_scr[...] = wu_ref[0].astype(BF16)

    @pl.when(i < na_ref[0])
    def _():
        x = x_ref[...].astype(BF16)
        gate = jnp.minimum(_dot(x, wg_scr[...]) + bg_ref[0], SWIGLU_LIMIT)
        up = jnp.clip(_dot(x, wu_scr[...]) + bu_ref[0], -SWIGLU_LIMIT, SWIGLU_LIMIT)
        h_ref[...] = ((up + 1.0) * gate * jax.nn.sigmoid(gate * SWIGLU_ALPHA)).astype(BF16)


def _moe_down_kernel(be_ref, na_ref, h_ref, wd_ref, bd_ref, y_ref, wd_scr):
    i = pl.program_id(1)
    fresh = jnp.logical_or(i == 0, be_ref[i] != be_ref[jnp.maximum(i - 1, 0)])

    @pl.when(jnp.logical_and(fresh, i < na_ref[0]))
    def _():
        wd_scr[...] = wd_ref[0].astype(BF16)

    @pl.when(i < na_ref[0])
    def _():
        y_ref[...] = _dot(h_ref[...], wd_scr[...]) + bd_ref[0]


def _moe_experts(xb, block_e, n_active, w_gu, b_gu, w_down, b_down):
    cap, d = xb.shape
    n_blocks = cap // MOE_BLOCK
    n_exp, _, ff2 = w_gu.shape
    ff = ff2 // 2
    tf = _pick_tile(ff, 1024)
    nf = ff // tf

    def blk(i, na):
        return jnp.minimum(i, na[0] - 1)

    h = pl.pallas_call(
        _moe_up_kernel,
        grid_spec=pltpu.PrefetchScalarGridSpec(
            num_scalar_prefetch=2,
            grid=(nf, n_blocks),
            in_specs=[pl.BlockSpec((MOE_BLOCK, d), lambda j, i, be, na: (blk(i, na), 0)),
                      pl.BlockSpec((1, d, tf), lambda j, i, be, na: (be[blk(i, na)], 0, j)),
                      pl.BlockSpec((1, d, tf), lambda j, i, be, na: (be[blk(i, na)], 0, nf + j)),
                      pl.BlockSpec((1, 1, tf), lambda j, i, be, na: (be[blk(i, na)], 0, j)),
                      pl.BlockSpec((1, 1, tf), lambda j, i, be, na: (be[blk(i, na)], 0, nf + j))],
            out_specs=pl.BlockSpec((MOE_BLOCK, tf), lambda j, i, be, na: (blk(i, na), j)),
            scratch_shapes=[pltpu.VMEM((d, tf), BF16), pltpu.VMEM((d, tf), BF16)]),
        out_shape=jax.ShapeDtypeStruct((cap, ff), BF16),
        compiler_params=_cparams(("arbitrary", "arbitrary")),
        name="moe_up",
    )(block_e, n_active, xb, w_gu, w_gu, b_gu.reshape(n_exp, 1, ff2), b_gu.reshape(n_exp, 1, ff2))

    tn = _pick_tile(d, 2048)
    return pl.pallas_call(
        _moe_down_kernel,
        grid_spec=pltpu.PrefetchScalarGridSpec(
            num_scalar_prefetch=2,
            grid=(d // tn, n_blocks),
            in_specs=[pl.BlockSpec((MOE_BLOCK, ff), lambda j, i, be, na: (blk(i, na), 0)),
                      pl.BlockSpec((1, ff, tn), lambda j, i, be, na: (be[blk(i, na)], 0, j)),
                      pl.BlockSpec((1, 1, tn), lambda j, i, be, na: (be[blk(i, na)], 0, j))],
            out_specs=pl.BlockSpec((MOE_BLOCK, tn), lambda j, i, be, na: (blk(i, na), j)),
            scratch_shapes=[pltpu.VMEM((ff, tn), BF16)]),
        out_shape=jax.ShapeDtypeStruct((cap, d), F32),
        compiler_params=_cparams(("arbitrary", "arbitrary")),
        name="moe_down",
    )(block_e, n_active, h, w_down, b_down.reshape(n_exp, 1, d))


def _route(logits, n_exp):
    t = logits.shape[0]
    top_logit, top_idx = lax.top_k(logits, TOP_K)
    gates = jax.nn.softmax(top_logit, axis=-1)
    n_slots = t * TOP_K
    flat_e = top_idx.reshape(-1)
    onehot = (flat_e[:, None] == jnp.arange(n_exp)[None, :]).astype(jnp.int32)
    rank = jnp.take_along_axis(jnp.cumsum(onehot, axis=0) - onehot, flat_e[:, None], axis=1)[:, 0]
    counts = jnp.sum(onehot, axis=0)
    padded = (counts + MOE_BLOCK - 1) // MOE_BLOCK * MOE_BLOCK
    pad_end = jnp.cumsum(padded)
    pad_start = pad_end - padded
    dest = (pad_start[flat_e] + rank).astype(jnp.int32)
    n_blocks = -(-n_slots // MOE_BLOCK) + n_exp
    cap = n_blocks * MOE_BLOCK
    src_tok = jnp.zeros((cap,), jnp.int32).at[dest].set(jnp.arange(n_slots, dtype=jnp.int32) // TOP_K)
    block_start = jnp.arange(n_blocks, dtype=jnp.int32) * MOE_BLOCK
    block_e = jnp.minimum(jnp.sum((block_start[:, None] >= pad_end[None, :]).astype(jnp.int32), axis=1),
                          n_exp - 1).astype(jnp.int32)
    n_active = (pad_end[-1] // MOE_BLOCK).astype(jnp.int32).reshape(1)
    return gates, dest.reshape(t, TOP_K), src_tok, block_e, n_active


def _final_kernel(x_ref, y_ref, gate_ref, gt_ref, g_ref, sh_ref, sc_ref, o_ref):
    gates = gate_ref[...]
    ffn = sum(y_ref[k] * gates[:, k:k + 1] for k in range(TOP_K))
    x = x_ref[...] + gt_ref[...] * ffn
    y = x * lax.rsqrt(jnp.mean(x * x, axis=-1, keepdims=True) + EPS) * g_ref[...]
    o_ref[...] = y * (1.0 + sc_ref[...]) + sh_ref[...]


def _final(x1, y_slots, gates, row0, gt, g, sh, sc):
    t, d = x1.shape
    tm = _pick_tile(t, 256)
    assert row0 % tm == 0
    row = lambda i: (i, 0)
    return pl.pallas_call(
        _final_kernel,
        grid=(t // tm,),
        in_specs=[pl.BlockSpec((tm, d), row),
                  pl.BlockSpec((TOP_K, tm, d), lambda i: (0, row0 // tm + i, 0)),
                  pl.BlockSpec((tm, TOP_K), row), _mod_spec(gt, tm),
                  pl.BlockSpec((1, d), lambda i: (0, 0)), _mod_spec(sh, tm), _mod_spec(sc, tm)],
        out_specs=pl.BlockSpec((tm, d), row),
        out_shape=jax.ShapeDtypeStruct((t, d), F32),
        compiler_params=_cparams(("arbitrary",)),
        name="final_norm",
    )(x1, y_slots, gates, gt, g, sh, sc)


def kernel(x_prompt, x_sample, cache_k, cache_v, state_gdn, state_conv, page_table, c_prompt, c_sample, w_ada, b_ada, norm1_g, norm2_g, w_in, sb_bias, conv_w, a_log, dt_bias, gdn_norm_g, w_out, w_router, b_router, w_gu, b_gu, w_down, b_down, w_ada_final, b_ada_final, norm_f_g):
    depth = w_ada.shape[0]
    assert depth == 1, "single-layer stack"
    nbp, seq, d = x_prompt.shape
    nbs, dec_seq, _ = x_sample.shape
    assert nbp == 1
    n_sb = cache_k.shape[-2]
    n_gdn = state_gdn.shape[2]
    width = n_sb * LANES
    assert cache_k.shape[-1] == LANES and n_gdn * LANES == width
    conv_taps = conv_w.shape[-1]
    n_exp = w_router.shape[-1]
    tp, ts = nbp * seq, nbs * dec_seq

    c_all = jnp.concatenate([c_prompt, c_sample], axis=0)
    n_c = c_all.shape[0]
    c_all = jnp.pad(c_all, ((0, -n_c % 8), (0, 0)))
    mods = _ada(c_all, w_ada[0], b_ada[0])[:n_c]
    fin = _ada(c_all, w_ada_final, b_ada_final)[:n_c]
    mods_p = [m for m in jnp.split(mods[:nbp], 6, axis=-1)]
    mods_s = [jnp.repeat(m, dec_seq, axis=0) for m in jnp.split(mods[nbp:], 6, axis=-1)]
    fin_p = jnp.split(fin[:nbp], 2, axis=-1)
    fin_s = [jnp.repeat(m, dec_seq, axis=0) for m in jnp.split(fin[nbp:], 2, axis=-1)]

    w_main = w_in[0][:, :7 * width].astype(BF16)
    w_ab = jnp.pad(w_in[0][:, 7 * width:], ((0, 0), (0, LANES - 2 * n_gdn))).astype(BF16)
    g1 = norm1_g[0].reshape(1, d)
    g2 = norm2_g[0].reshape(1, d)
    conv_wt = conv_w[0].T
    w_oa = w_out[0][:width].astype(BF16)
    w_ob = w_out[0][width:].astype(BF16)
    w_r = jnp.pad(w_router[0], ((0, 0), (0, LANES - n_exp)))
    b_r = jnp.pad(b_router[0], (0, LANES - n_exp), constant_values=-1e30).reshape(1, LANES)
    ng = gdn_norm_g[0].reshape(1, LANES)

    def mixer(x, m, sba_fn, conv0, s0, n_seq, chunk):
        sh1, sc1, gt1, sh2, sc2, _ = m
        proj, gab = _inproj(x, g1, sh1, sc1, w_main, w_ab, width)
        o_sb = sba_fn(proj)
        o_gdn, s_new = _gdn(proj, gab, conv_wt, conv0, s0, a_log[0], dt_bias[0], ng, n_seq, chunk)
        x1, hffn, logits = _outproj(o_sb, o_gdn, x, w_oa, w_ob, gt1, g2, sh2, sc2, w_r, b_r)
        return proj, s_new, x1, hffn, logits[:, :n_exp]

    def conv_layout(c0):
        n = c0.shape[0]
        return c0.reshape(n, conv_taps - 1, 3, width).transpose(0, 2, 1, 3)

    xp = x_prompt.reshape(tp, d)
    xs = x_sample.reshape(ts, d)
    conv0_p = jnp.zeros((nbp, 3, conv_taps - 1, width), F32)
    s0_p = jnp.zeros((nbp, n_gdn, LANES, LANES), F32)
    proj_p, s_p, x1_p, h_p, lg_p = mixer(
        xp, mods_p, lambda pr: _sba_prompt(pr, sb_bias[0], n_sb), conv0_p, s0_p, nbp, GDN_CHUNK)
    k_pool = cache_k.reshape(cache_k.shape[1], cache_k.shape[2] * n_sb, LANES)
    v_pool = cache_v.reshape(cache_v.shape[1], cache_v.shape[2] * n_sb, LANES)
    proj_s, s_s, x1_s, h_s, lg_s = mixer(
        xs, mods_s, lambda pr: _sba_paged(pr, sb_bias[0], k_pool, v_pool, page_table, n_sb, dec_seq),
        conv_layout(state_conv[0]), state_gdn[0], nbs, dec_seq)

    hffn = jnp.concatenate([h_p, h_s], axis=0)
    logits = jnp.concatenate([lg_p, lg_s], axis=0)
    gates, dest, src_tok, block_e, n_active = _route(logits, n_exp)
    xb = hffn[src_tok]
    y = _moe_experts(xb, block_e, n_active, w_gu[0], b_gu[0], w_down[0], b_down[0])
    y_slots = y[dest.T.reshape(-1)].reshape(TOP_K, tp + ts, d)

    gf = norm_f_g.reshape(1, d)
    y_p = _final(x1_p, y_slots, gates[:tp], 0, mods_p[5], gf, fin_p[0], fin_p[1])
    y_s = _final(x1_s, y_slots, gates[tp:], tp, mods_s[5], gf, fin_s[0], fin_s[1])

    def new_conv(proj, n_seq):
        t = proj.shape[1]
        tail = proj[3:6].reshape(3, n_seq, t // n_seq, width)[:, :, -(conv_taps - 1):]
        return tail.transpose(1, 2, 0, 3).reshape(n_seq, conv_taps - 1, 3 * width)

    heads = lambda a, n, l: a.reshape(1, n, l, n_sb, LANES)
    return (y_p.reshape(nbp, seq, d), y_s.reshape(nbs, dec_seq, d),
            heads(proj_p[1], nbp, seq), heads(proj_p[2], nbp, seq),
            heads(proj_s[1], nbs, dec_seq), heads(proj_s[2], nbs, dec_seq),
            s_p[None], s_s[None], new_conv(proj_p, nbp)[None], new_conv(proj_s, nbs)[None])
```

```python
import functools

import jax
import jax.numpy as jnp
from jax import lax
from jax.experimental import pallas as pl
from jax.experimental.pallas import tpu as pltpu

F32 = jnp.float32
BF16 = jnp.bfloat16

LANES = 128
EPS = 1e-6
TOP_K = 4
SWIGLU_LIMIT = 7.0
SWIGLU_ALPHA = 1.702
MOE_BLOCK = 256
GDN_CHUNK = 64
SB_BLOCK = 256
SB_TILE_BLOCKS = 4
VMEM_LIMIT = 56 * 1024 * 1024
HIGHEST = lax.Precision.HIGHEST
LOG2E = 1.4426950408889634


def _cparams(sem):
    return pltpu.CompilerParams(dimension_semantics=sem, vmem_limit_bytes=VMEM_LIMIT)


def _silu(x):
    return x * jax.nn.sigmoid(x)


def _softplus(x):
    return jnp.maximum(x, 0.0) + jnp.log1p(jnp.exp(-jnp.abs(x)))


def _dot(a, b, precision=None):
    return jnp.dot(a, b, preferred_element_type=F32, precision=precision)


def _dot_nt(a, b, precision=None):
    return lax.dot_general(a, b, (((1,), (1,)), ((), ())), preferred_element_type=F32, precision=precision)


def _dot_tn(a, b, precision=None):
    return lax.dot_general(a, b, (((0,), (0,)), ((), ())), preferred_element_type=F32, precision=precision)


def _pick_tile(n, target):
    t = min(n, target)
    while n % t:
        t //= 2
    return t


def _ada_kernel(c_ref, w_ref, b_ref, o_ref):
    a = _silu(c_ref[...]).astype(BF16)
    o_ref[...] = _dot(a, w_ref[...].astype(BF16)) + b_ref[...]


def _ada(c_all, w, b):
    r, d = c_all.shape
    n = w.shape[1]
    tn = _pick_tile(n, 1024)
    return pl.pallas_call(
        _ada_kernel,
        grid=(n // tn,),
        in_specs=[pl.BlockSpec((r, d), lambda j: (0, 0)),
                  pl.BlockSpec((d, tn), lambda j: (0, j)),
                  pl.BlockSpec((1, tn), lambda j: (0, j))],
        out_specs=pl.BlockSpec((r, tn), lambda j: (0, j)),
        out_shape=jax.ShapeDtypeStruct((r, n), F32),
        compiler_params=_cparams(("arbitrary",)),
        name="ada_mod",
    )(c_all, w, b.reshape(1, n))


def _inproj_kernel(x_ref, g_ref, sh_ref, sc_ref, w_ref, wab_ref, proj_ref, gab_ref, h_scr):
    @pl.when(pl.program_id(1) == 0)
    def _():
        x = x_ref[...]
        y = x * lax.rsqrt(jnp.mean(x * x, axis=-1, keepdims=True) + EPS) * g_ref[...]
        h = (y * (1.0 + sc_ref[...]) + sh_ref[...]).astype(BF16)
        h_scr[...] = h
        gab_ref[...] = _dot(h, wab_ref[...])

    proj_ref[0] = _dot(h_scr[...], w_ref[...])


def _mod_spec(arr, tm):
    d = arr.shape[1]
    if arr.shape[0] == 1:
        return pl.BlockSpec((1, d), lambda i, *_: (0, 0))
    return pl.BlockSpec((tm, d), lambda i, *_: (i, 0))


def _inproj(x, g, sh, sc, w_main, w_ab, width):
    t, d = x.shape
    planes = w_main.shape[1] // width
    tm = _pick_tile(t, 512)
    return pl.pallas_call(
        _inproj_kernel,
        grid=(t // tm, planes),
        in_specs=[pl.BlockSpec((tm, d), lambda i, j: (i, 0)),
                  pl.BlockSpec((1, d), lambda i, j: (0, 0)),
                  _mod_spec(sh, tm), _mod_spec(sc, tm),
                  pl.BlockSpec((d, width), lambda i, j: (0, j)),
                  pl.BlockSpec((d, LANES), lambda i, j: (0, 0))],
        out_specs=[pl.BlockSpec((1, tm, width), lambda i, j: (j, i, 0)),
                   pl.BlockSpec((tm, LANES), lambda i, j: (i, 0))],
        out_shape=[jax.ShapeDtypeStruct((planes, t, width), F32),
                   jax.ShapeDtypeStruct((t, LANES), F32)],
        scratch_shapes=[pltpu.VMEM((tm, d), BF16)],
        compiler_params=_cparams(("arbitrary", "arbitrary")),
        name="in_proj",
    )(x, g, sh, sc, w_main, w_ab)


def _cum_matrix(blk):
    j = lax.broadcasted_iota(jnp.int32, (blk, blk + LANES), 0)
    s = lax.broadcasted_iota(jnp.int32, (blk, blk + LANES), 1)
    return jnp.where((j >= s) | (s >= blk), 1.0, 0.0).astype(BF16)


def _sb_block(zz, v, valid, cum_mat, blk):
    y = jnp.maximum(zz, 0.0) + jnp.log2(1.0 + jnp.exp2(-jnp.abs(zz)))
    if valid is not None:
        y = jnp.where(valid, y, 0.0)
    su = _dot(y.astype(BF16), cum_mat)
    p = jnp.exp2(zz - su[:, :blk])
    if valid is not None:
        p = jnp.where(valid, p, 0.0)
    return _dot(p.astype(BF16), v), su[:, blk:]


def _sba_prompt_kernel(bias_ref, q_ref, k_ref, v_ref, o_ref, q_scr, k_scr, v_scr, acc_scr, lsum_scr,
                       *, scale2, blk, ns):
    h = pl.program_id(0)
    qi = pl.program_id(1)
    tq = ns * blk

    @pl.when(qi == 0)
    def _():
        k_scr[...] = k_ref[0].astype(BF16)
        v_scr[...] = v_ref[0].astype(BF16)

    q_scr[...] = (q_ref[0] * scale2).astype(BF16)
    acc_scr[...] = jnp.zeros_like(acc_scr)
    lsum_scr[...] = jnp.zeros_like(lsum_scr)
    bias2 = bias_ref[h] * LOG2E
    cum_mat = _cum_matrix(blk)

    def visit(kb, row0, valid):
        start = pl.multiple_of(kb * blk, blk)
        zz = _dot_nt(q_scr[row0:tq, :], k_scr[pl.ds(start, blk), :]) + bias2
        pv, tot = _sb_block(zz, v_scr[pl.ds(start, blk), :], valid, cum_mat, blk)
        lsum = lsum_scr[row0:tq, :]
        acc_scr[row0:tq, :] += jnp.exp2(-lsum) * pv
        lsum_scr[row0:tq, :] = lsum + tot

    for j in reversed(range(ns)):
        row = lax.broadcasted_iota(jnp.int32, (tq - j * blk, blk), 0)
        col = lax.broadcasted_iota(jnp.int32, (tq - j * blk, blk), 1)
        visit(qi * ns + j, j * blk, col < row)

    step = 2 if ns % 2 == 0 else 1

    def earlier(t, carry):
        for u in range(step):
            visit(qi * ns - 1 - (t * step + u), 0, None)
        return carry

    lax.fori_loop(0, qi * (ns // step), earlier, 0)
    o_ref[...] = acc_scr[...]


def _sba_prompt(proj, bias, n_heads):
    _, l, w = proj.shape
    blk = SB_BLOCK
    ns = _pick_tile(l // blk, SB_TILE_BLOCKS)
    tq = ns * blk
    kern = functools.partial(_sba_prompt_kernel, scale2=LANES ** -0.5 * LOG2E, blk=blk, ns=ns)
    return pl.pallas_call(
        kern,
        grid=(n_heads, l // tq),
        in_specs=[pl.BlockSpec(memory_space=pltpu.SMEM),
                  pl.BlockSpec((1, tq, LANES), lambda h, i: (0, i, h)),
                  pl.BlockSpec((1, l, LANES), lambda h, i: (1, 0, h)),
                  pl.BlockSpec((1, l, LANES), lambda h, i: (2, 0, h))],
        out_specs=pl.BlockSpec((tq, LANES), lambda h, i: (i, h)),
        out_shape=jax.ShapeDtypeStruct((l, w), F32),
        scratch_shapes=[pltpu.VMEM((tq, LANES), BF16), pltpu.VMEM((l, LANES), BF16), pltpu.VMEM((l, LANES), BF16),
                        pltpu.VMEM((tq, LANES), F32), pltpu.VMEM((tq, LANES), F32)],
        compiler_params=_cparams(("arbitrary", "arbitrary")),
        name="sba_prompt",
    )(bias, proj, proj, proj)


def _sba_paged_kernel(pt_ref, bias_ref, q_ref, kn_ref, vn_ref, *rest, n_heads, s, scale2, group):
    kp_refs = rest[:group]
    vp_refs = rest[group:2 * group]
    o_ref, wq_scr, bias_scr, acc_scr, lsum_scr = rest[2 * group:]
    p = pl.program_id(1)
    page = kp_refs[0].shape[1] // n_heads
    hq = n_heads * s
    w = n_heads * LANES
    cum_mat = _cum_matrix(page)

    def visit(ks, vs, valid):
        n = len(ks)
        zz = _dot_nt(wq_scr[...], jnp.concatenate(ks, axis=0))
        zz = jnp.concatenate([zz[:, g * page:(g + 1) * page] for g in range(n)], axis=0) + \
            jnp.concatenate([bias_scr[...]] * n, axis=0)
        y = jnp.maximum(zz, 0.0) + jnp.log2(1.0 + jnp.exp2(-jnp.abs(zz)))
        if valid is not None:
            y = jnp.where(valid, y, 0.0)
        su = _dot(y.astype(BF16), cum_mat)
        p = jnp.exp2(zz - su[:, :page])
        if valid is not None:
            p = jnp.where(valid, p, 0.0)
        p = p.astype(BF16)
        acc = acc_scr[...]
        lsum = lsum_scr[...]
        for g in range(n):
            pv = _dot(p[g * hq:(g + 1) * hq], vs[g])
            pv = jnp.concatenate([pv[h * s:(h + 1) * s, h * LANES:(h + 1) * LANES] for h in range(n_heads)], axis=0)
            acc = acc + jnp.exp2(-lsum) * pv
            lsum = lsum + su[g * hq:(g + 1) * hq, page:]
        acc_scr[...] = acc
        lsum_scr[...] = lsum

    @pl.when(p == 0)
    def _():
        qt = jnp.concatenate([q_ref[0] * scale2] * n_heads, axis=0)
        r2 = lax.broadcasted_iota(jnp.int32, (hq, w), 0)
        c2 = lax.broadcasted_iota(jnp.int32, (hq, w), 1)
        wq_scr[...] = jnp.where(r2 // s == c2 // LANES, qt, 0.0).astype(BF16)
        row = lax.broadcasted_iota(jnp.int32, (hq, page), 0)
        col = lax.broadcasted_iota(jnp.int32, (hq, page), 1)
        bias = jnp.zeros((hq, page), F32)
        for h in range(n_heads):
            bias = jnp.where(row // s == h, bias_ref[h] * LOG2E, bias)
        bias_scr[...] = bias
        acc_scr[...] = jnp.zeros_like(acc_scr)
        lsum_scr[...] = jnp.zeros_like(lsum_scr)
        pad = jnp.zeros((page - s, w), F32)
        kn = jnp.concatenate([kn_ref[0], pad], axis=0).astype(BF16)
        vn = jnp.concatenate([vn_ref[0], pad], axis=0).astype(BF16)
        visit([kn], [vn], col < row % s)

    def head_major(ref):
        return jnp.concatenate([ref[0, pl.ds(h, page, stride=n_heads), :] for h in range(n_heads)],
                               axis=1).astype(BF16)

    visit([head_major(r) for r in kp_refs], [head_major(r) for r in vp_refs], None)

    @pl.when(p == pl.num_programs(1) - 1)
    def _():
        for h in range(n_heads):
            o_ref[:, h * LANES:(h + 1) * LANES] = acc_scr[h * s:(h + 1) * s, :]


def _sba_paged(proj, bias, k_pool, v_pool, page_table, n_heads, s):
    _, t, w = proj.shape
    nb, n_pages = page_table.shape
    rows = k_pool.shape[1]
    page = rows // n_heads
    hq = n_heads * s
    group = _pick_tile(n_pages, 8)
    kern = functools.partial(_sba_paged_kernel, n_heads=n_heads, s=s, scale2=LANES ** -0.5 * LOG2E, group=group)

    def pool_spec(g):
        return pl.BlockSpec((1, rows, LANES), lambda b, p, pt: (pt[b, n_pages - 1 - (p * group + g)], 0, 0))

    pool_specs = [pool_spec(g) for g in range(group)]
    return pl.pallas_call(
        kern,
        grid_spec=pltpu.PrefetchScalarGridSpec(
            num_scalar_prefetch=1,
            grid=(nb, n_pages // group),
            in_specs=[pl.BlockSpec(memory_space=pltpu.SMEM),
                      pl.BlockSpec((1, s, w), lambda b, p, pt: (0, b, 0)),
                      pl.BlockSpec((1, s, w), lambda b, p, pt: (1, b, 0)),
                      pl.BlockSpec((1, s, w), lambda b, p, pt: (2, b, 0))] + pool_specs + pool_specs,
            out_specs=pl.BlockSpec((s, w), lambda b, p, pt: (b, 0)),
            scratch_shapes=[pltpu.VMEM((hq, w), BF16), pltpu.VMEM((hq, page), F32),
                            pltpu.VMEM((hq, LANES), F32), pltpu.VMEM((hq, page), F32)]),
        out_shape=jax.ShapeDtypeStruct((t, w), F32),
        compiler_params=_cparams(("arbitrary", "arbitrary")),
        name="sba_paged",
    )(page_table, bias, proj, proj, proj, *([k_pool] * group), *([v_pool] * group))


_NN = ((1,), (0,))
_NT = ((1,), (1,))


def _split(a, parts):
    out = []
    for _ in range(parts - 1):
        hi = a.astype(BF16)
        out.append(hi)
        a = a - hi.astype(F32)
    out.append(a.astype(BF16))
    return out


def _mm(a, b, dims, split):
    if not split:
        return lax.dot_general(a, b, (dims, ((), ())), preferred_element_type=F32, precision=HIGHEST)
    d = lambda x, y: lax.dot_general(x, y, (dims, ((), ())), preferred_element_type=F32)
    return d(a[1], b[0]) + d(a[0], b[1]) + d(a[0], b[0])


def _gdn_chunk(q, k, v, g_col, beta, state, c):
    heads = range(len(q))
    split = c >= 16
    sp = (lambda x: _split(x, 2)) if split else (lambda x: x)
    ri = lax.broadcasted_iota(jnp.int32, (c, c), 0)
    cj = lax.broadcasted_iota(jnp.int32, (c, c), 1)
    eye = (ri == cj).astype(F32)
    g_row = [jnp.sum(jnp.where(ri == cj, g_col[h], 0.0), axis=0, keepdims=True) for h in heads]
    if split:
        incl = (cj <= ri).astype(BF16)
        incl_t = (ri <= cj).astype(BF16)
        gc_col = [sum(_dot(incl, part) for part in reversed(_split(jnp.broadcast_to(g_col[h], (c, LANES)), 3)))
                  for h in heads]
        gc_row = [sum(_dot(part, incl_t) for part in reversed(_split(jnp.broadcast_to(g_row[h], (16, c)), 3)))[0:1]
                  for h in heads]
    else:
        gc_col = [_dot((cj <= ri).astype(F32), jnp.broadcast_to(g_col[h], (c, LANES)), HIGHEST) for h in heads]
        gc_row = [_dot(jnp.broadcast_to(g_row[h], (8, c)), (ri <= cj).astype(F32), HIGHEST)[0:1] for h in heads]
    decay = [jnp.where(cj <= ri, jnp.exp(gc_col[h][:, 0:1] - gc_row[h]), 0.0) for h in heads]
    kb = [k[h] * beta[h] for h in heads]
    k_s = [sp(k[h]) for h in heads]
    kk = [_mm(sp(kb[h]), k_s[h], _NT, split) for h in heads]
    qk = [_mm(sp(q[h]), k_s[h], _NT, split) * decay[h] for h in heads]
    n = [jnp.where(cj < ri, -kk[h] * decay[h], 0.0) for h in heads]
    inv = [eye + n[h] for h in heads]
    i = 2
    while i < c:
        n_s = [sp(n[h]) for h in heads]
        n = [_mm(n_s[h], n_s[h], _NN, split) for h in heads]
        inv = [_mm(sp(inv[h]), sp(eye + n[h]), _NN, split) for h in heads]
        i *= 2
    rhs = [jnp.concatenate([v[h] * beta[h], kb[h] * jnp.exp(gc_col[h])], axis=1) for h in heads]
    sol = [_mm(sp(inv[h]), sp(rhs[h]), _NN, split) for h in heads]
    state_s = [sp(state[h]) for h in heads]
    ws = [_mm(sp(sol[h][:, LANES:]), state_s[h], _NN, split) for h in heads]
    qs = [_mm(sp(q[h] * jnp.exp(gc_col[h])), state_s[h], _NN, split) for h in heads]
    v_new = [sol[h][:, :LANES] - ws[h] for h in heads]
    g_last = [gc_col[h][c - 1:c, :] for h in heads]
    k_dec = [k[h] * jnp.exp(g_last[h] - gc_col[h]) for h in heads]
    o = [qs[h] + _mm(sp(qk[h]), sp(v_new[h]), _NN, split) for h in heads]
    new_state = [state[h] * jnp.exp(g_last[h]) + _dot_tn(k_dec[h], v_new[h], HIGHEST) for h in heads]
    return o, new_state


def _gdn_kernel(alog_ref, dtb_ref, q_ref, k_ref, v_ref, z_ref, gab_ref, cw_ref, c0_ref, s0_ref, ng_ref,
                o_ref, s_ref, tail_scr, state_scr, *, c, n_heads, conv_w):
    ci = pl.program_id(1)
    halo = 8
    w = n_heads * LANES

    @pl.when(ci == 0)
    def _():
        state_scr[...] = s0_ref[0]
        for a in range(3):
            tail_scr[a, 0:halo, :] = jnp.zeros((halo, w), F32)
            tail_scr[a, halo - (conv_w - 1):halo, :] = c0_ref[0, a]

    def conv(a, x_ref):
        tail_scr[a, halo:halo + c, :] = x_ref[0]
        lo = halo - (conv_w - 1)
        y = sum(tail_scr[a, lo + i:lo + i + c, :] * cw_ref[i:i + 1, a * w:(a + 1) * w] for i in range(conv_w))
        tail_scr[a, 0:halo, :] = tail_scr[a, c:c + halo, :]
        return _silu(y)

    q_all = conv(0, q_ref)
    k_all = conv(1, k_ref)
    v_all = conv(2, v_ref)
    gab = gab_ref[...]
    lane = lax.broadcasted_iota(jnp.int32, gab.shape, 1)

    heads = range(n_heads)
    sl = [slice(h * LANES, (h + 1) * LANES) for h in heads]
    q = [q_all[:, sl[h]] for h in heads]
    k = [k_all[:, sl[h]] for h in heads]
    q = [q[h] * lax.rsqrt(jnp.sum(q[h] * q[h], axis=-1, keepdims=True) + EPS) * (LANES ** -0.5) for h in heads]
    k = [k[h] * lax.rsqrt(jnp.sum(k[h] * k[h], axis=-1, keepdims=True) + EPS) for h in heads]
    ga = [jnp.sum(jnp.where(lane == h, gab, 0.0), axis=-1, keepdims=True) for h in heads]
    gb = [jnp.sum(jnp.where(lane == h + n_heads, gab, 0.0), axis=-1, keepdims=True) for h in heads]
    g_col = [-jnp.exp(alog_ref[h]) * _softplus(ga[h] + dtb_ref[h]) for h in heads]
    beta = [jax.nn.sigmoid(gb[h]) for h in heads]
    o, state = _gdn_chunk(q, k, [v_all[:, sl[h]] for h in heads], g_col, beta, [state_scr[h] for h in heads], c)
    for h in heads:
        state_scr[h] = state[h]
        on = o[h] * lax.rsqrt(jnp.mean(o[h] * o[h], axis=-1, keepdims=True) + EPS) * ng_ref[...]
        o_ref[:, sl[h]] = on * _silu(z_ref[0, :, sl[h]])

    @pl.when(ci == pl.num_programs(1) - 1)
    def _():
        s_ref[0] = state_scr[...]


def _gdn(proj, gab, conv_wt, conv0, s0, a_log, dt_bias, norm_g, n_seq, c):
    _, t, w = proj.shape
    n_heads = w // LANES
    l = t // n_seq
    nc = l // c
    conv_w = conv_wt.shape[0]
    kern = functools.partial(_gdn_kernel, c=c, n_heads=n_heads, conv_w=conv_w)

    def tok(plane):
        return pl.BlockSpec((1, c, w), lambda n, ci: (plane, n * nc + ci, 0))

    smem = pl.BlockSpec(memory_space=pltpu.SMEM)
    return pl.pallas_call(
        kern,
        grid=(n_seq, nc),
        in_specs=[smem, smem, tok(3), tok(4), tok(5), tok(6),
                  pl.BlockSpec((c, LANES), lambda n, ci: (n * nc + ci, 0)),
                  pl.BlockSpec((conv_w, 3 * w), lambda n, ci: (0, 0)),
                  pl.BlockSpec((1, 3, conv_w - 1, w), lambda n, ci: (n, 0, 0, 0)),
                  pl.BlockSpec((1, n_heads, LANES, LANES), lambda n, ci: (n, 0, 0, 0)),
                  pl.BlockSpec((1, LANES), lambda n, ci: (0, 0))],
        out_specs=[pl.BlockSpec((c, w), lambda n, ci: (n * nc + ci, 0)),
                   pl.BlockSpec((1, n_heads, LANES, LANES), lambda n, ci: (n, 0, 0, 0))],
        out_shape=[jax.ShapeDtypeStruct((t, w), F32),
                   jax.ShapeDtypeStruct((n_seq, n_heads, LANES, LANES), F32)],
        scratch_shapes=[pltpu.VMEM((3, c + 8, w), F32), pltpu.VMEM((n_heads, LANES, LANES), F32)],
        compiler_params=_cparams(("arbitrary", "arbitrary")),
        name="gdn",
    )(a_log, dt_bias, proj, proj, proj, proj, gab, conv_wt, conv0, s0, norm_g)


def _outproj_kernel(osb_ref, ogdn_ref, x_ref, wa_ref, wb_ref, gt_ref, g_ref, sh_ref, sc_ref, wr_ref, br_ref,
                    x1_ref, h_ref, lg_ref):
    mix = _dot(osb_ref[...].astype(BF16), wa_ref[...]) + _dot(ogdn_ref[...].astype(BF16), wb_ref[...])
    x1 = x_ref[...] + gt_ref[...] * mix
    x1_ref[...] = x1
    y = x1 * lax.rsqrt(jnp.mean(x1 * x1, axis=-1, keepdims=True) + EPS) * g_ref[...]
    hf = y * (1.0 + sc_ref[...]) + sh_ref[...]
    h_ref[...] = hf
    lg_ref[...] = _dot(hf, wr_ref[...], HIGHEST) + br_ref[...]


def _outproj(o_sb, o_gdn, x, w_a, w_b, gt, g, sh, sc, w_r, b_r):
    t, d = x.shape
    w = o_sb.shape[1]
    tm = _pick_tile(t, 256)
    row = lambda i: (i, 0)
    fixed = lambda i: (0, 0)
    return pl.pallas_call(
        _outproj_kernel,
        grid=(t // tm,),
        in_specs=[pl.BlockSpec((tm, w), row), pl.BlockSpec((tm, w), row), pl.BlockSpec((tm, d), row),
                  pl.BlockSpec((w, d), fixed), pl.BlockSpec((w, d), fixed),
                  _mod_spec(gt, tm), pl.BlockSpec((1, d), fixed), _mod_spec(sh, tm), _mod_spec(sc, tm),
                  pl.BlockSpec((d, LANES), fixed), pl.BlockSpec((1, LANES), fixed)],
        out_specs=[pl.BlockSpec((tm, d), row), pl.BlockSpec((tm, d), row), pl.BlockSpec((tm, LANES), row)],
        out_shape=[jax.ShapeDtypeStruct((t, d), F32), jax.ShapeDtypeStruct((t, d), F32),
                   jax.ShapeDtypeStruct((t, LANES), F32)],
        compiler_params=_cparams(("arbitrary",)),
        name="out_proj",
    )(o_sb, o_gdn, x, w_a, w_b, gt, g, sh, sc, w_r, b_r)


def _moe_up_kernel(be_ref, na_ref, x_ref, wg_ref, wu_ref, bg_ref, bu_ref, h_ref, wg_scr, wu_scr):
    i = pl.program_id(1)
    fresh = jnp.logical_or(i == 0, be_ref[i] != be_ref[jnp.maximum(i - 1, 0)])

    @pl.when(jnp.logical_and(fresh, i < na_ref[0]))
    def _():
        wg_scr[...] = wg_ref[0].astype(BF16)
        wu_scr[...] = wu_ref[0].astype(BF16)

    @pl.when(i < na_ref[0])
    def _():
        x = x_ref[...].astype(BF16)
        gate = jnp.minimum(_dot(x, wg_scr[...]) + bg_ref[0], SWIGLU_LIMIT)
        up = jnp.clip(_dot(x, wu_scr[...]) + bu_ref[0], -SWIGLU_LIMIT, SWIGLU_LIMIT)
        h_ref[...] = ((up + 1.0) * gate * jax.nn.sigmoid(gate * SWIGLU_ALPHA)).astype(BF16)


def _moe_down_kernel(be_ref, na_ref, h_ref, wd_ref, bd_ref, y_ref, wd_scr):
    i = pl.program_id(1)
    fresh = jnp.logical_or(i == 0, be_ref[i] != be_ref[jnp.maximum(i - 1, 0)])

    @pl.when(jnp.logical_and(fresh, i < na_ref[0]))
    def _():
        wd_scr[...] = wd_ref[0].astype(BF16)

    @pl.when(i < na_ref[0])
    def _():
        y_ref[...] = _dot(h_ref[...], wd_scr[...]) + bd_ref[0]


def _moe_experts(xb, block_e, n_active, w_gu, b_gu, w_down, b_down):
    cap, d = xb.shape
    n_blocks = cap // MOE_BLOCK
    n_exp, _, ff2 = w_gu.shape
    ff = ff2 // 2
    tf = _pick_tile(ff, 1024)
    nf = ff // tf

    def blk(i, na):
        return jnp.minimum(i, na[0] - 1)

    h = pl.pallas_call(
        _moe_up_kernel,
        grid_spec=pltpu.PrefetchScalarGridSpec(
            num_scalar_prefetch=2,
            grid=(nf, n_blocks),
            in_specs=[pl.BlockSpec((MOE_BLOCK, d), lambda j, i, be, na: (blk(i, na), 0)),
                      pl.BlockSpec((1, d, tf), lambda j, i, be, na: (be[blk(i, na)], 0, j)),
                      pl.BlockSpec((1, d, tf), lambda j, i, be, na: (be[blk(i, na)], 0, nf + j)),
                      pl.BlockSpec((1, 1, tf), lambda j, i, be, na: (be[blk(i, na)], 0, j)),
                      pl.BlockSpec((1, 1, tf), lambda j, i, be, na: (be[blk(i, na)], 0, nf + j))],
            out_specs=pl.BlockSpec((MOE_BLOCK, tf), lambda j, i, be, na: (blk(i, na), j)),
            scratch_shapes=[pltpu.VMEM((d, tf), BF16), pltpu.VMEM((d, tf), BF16)]),
        out_shape=jax.ShapeDtypeStruct((cap, ff), BF16),
        compiler_params=_cparams(("arbitrary", "arbitrary")),
        name="moe_up",
    )(block_e, n_active, xb, w_gu, w_gu, b_gu.reshape(n_exp, 1, ff2), b_gu.reshape(n_exp, 1, ff2))

    tn = _pick_tile(d, 2048)
    return pl.pallas_call(
        _moe_down_kernel,
        grid_spec=pltpu.PrefetchScalarGridSpec(
            num_scalar_prefetch=2,
            grid=(d // tn, n_blocks),
            in_specs=[pl.BlockSpec((MOE_BLOCK, ff), lambda j, i, be, na: (blk(i, na), 0)),
                      pl.BlockSpec((1, ff, tn), lambda j, i, be, na: (be[blk(i, na)], 0, j)),
                      pl.BlockSpec((1, 1, tn), lambda j, i, be, na: (be[blk(i, na)], 0, j))],
            out_specs=pl.BlockSpec((MOE_BLOCK, tn), lambda j, i, be, na: (blk(i, na), j)),
            scratch_shapes=[pltpu.VMEM((ff, tn), BF16)]),
        out_shape=jax.ShapeDtypeStruct((cap, d), F32),
        compiler_params=_cparams(("arbitrary", "arbitrary")),
        name="moe_down",
    )(block_e, n_active, h, w_down, b_down.reshape(n_exp, 1, d))


def _route(logits, n_exp):
    t = logits.shape[0]
    top_logit, top_idx = lax.top_k(logits, TOP_K)
    gates = jax.nn.softmax(top_logit, axis=-1)
    n_slots = t * TOP_K
    flat_e = top_idx.reshape(-1)
    onehot = (flat_e[:, None] == jnp.arange(n_exp)[None, :]).astype(jnp.int32)
    rank = jnp.take_along_axis(jnp.cumsum(onehot, axis=0) - onehot, flat_e[:, None], axis=1)[:, 0]
    counts = jnp.sum(onehot, axis=0)
    padded = (counts + MOE_BLOCK - 1) // MOE_BLOCK * MOE_BLOCK
    pad_end = jnp.cumsum(padded)
    pad_start = pad_end - padded
    dest = (pad_start[flat_e] + rank).astype(jnp.int32)
    n_blocks = -(-n_slots // MOE_BLOCK) + n_exp
    cap = n_blocks * MOE_BLOCK
    src_tok = jnp.zeros((cap,), jnp.int32).at[dest].set(jnp.arange(n_slots, dtype=jnp.int32) // TOP_K)
    block_start = jnp.arange(n_blocks, dtype=jnp.int32) * MOE_BLOCK
    block_e = jnp.minimum(jnp.sum((block_start[:, None] >= pad_end[None, :]).astype(jnp.int32), axis=1),
                          n_exp - 1).astype(jnp.int32)
    n_active = (pad_end[-1] // MOE_BLOCK).astype(jnp.int32).reshape(1)
    return gates, dest.reshape(t, TOP_K), src_tok, block_e, n_active


def _final_kernel(x_ref, y_ref, gate_ref, gt_ref, g_ref, sh_ref, sc_ref, o_ref):
    gates = gate_ref[...]
    ffn = sum(y_ref[k] * gates[:, k:k + 1] for k in range(TOP_K))
    x = x_ref[...] + gt_ref[...] * ffn
    y = x * lax.rsqrt(jnp.mean(x * x, axis=-1, keepdims=True) + EPS) * g_ref[...]
    o_ref[...] = y * (1.0 + sc_ref[...]) + sh_ref[...]


def _final(x1, y_slots, gates, row0, gt, g, sh, sc):
    t, d = x1.shape
    tm = _pick_tile(t, 256)
    assert row0 % tm == 0
    row = lambda i: (i, 0)
    return pl.pallas_call(
        _final_kernel,
        grid=(t // tm,),
        in_specs=[pl.BlockSpec((tm, d), row),
                  pl.BlockSpec((TOP_K, tm, d), lambda i: (0, row0 // tm + i, 0)),
                  pl.BlockSpec((tm, TOP_K), row), _mod_spec(gt, tm),
                  pl.BlockSpec((1, d), lambda i: (0, 0)), _mod_spec(sh, tm), _mod_spec(sc, tm)],
        out_specs=pl.BlockSpec((tm, d), row),
        out_shape=jax.ShapeDtypeStruct((t, d), F32),
        compiler_params=_cparams(("arbitrary",)),
        name="final_norm",
    )(x1, y_slots, gates, gt, g, sh, sc)


def kernel(x_prompt, x_sample, cache_k, cache_v, state_gdn, state_conv, page_table, c_prompt, c_sample, w_ada, b_ada, norm1_g, norm2_g, w_in, sb_bias, conv_w, a_log, dt_bias, gdn_norm_g, w_out, w_router, b_router, w_gu, b_gu, w_down, b_down, w_ada_final, b_ada_final, norm_f_g):
    depth = w_ada.shape[0]
    assert depth == 1, "single-layer stack"
    nbp, seq, d = x_prompt.shape
    nbs, dec_seq, _ = x_sample.shape
    assert nbp == 1
    n_sb = cache_k.shape[-2]
    n_gdn = state_gdn.shape[2]
    width = n_sb * LANES
    assert cache_k.shape[-1] == LANES and n_gdn * LANES == width
    conv_taps = conv_w.shape[-1]
    n_exp = w_router.shape[-1]
    tp, ts = nbp * seq, nbs * dec_seq

    c_all = jnp.concatenate([c_prompt, c_sample], axis=0)
    n_c = c_all.shape[0]
    c_all = jnp.pad(c_all, ((0, -n_c % 8), (0, 0)))
    mods = _ada(c_all, w_ada[0], b_ada[0])[:n_c]
    fin = _ada(c_all, w_ada_final, b_ada_final)[:n_c]
    mods_p = [m for m in jnp.split(mods[:nbp], 6, axis=-1)]
    mods_s = [jnp.repeat(m, dec_seq, axis=0) for m in jnp.split(mods[nbp:], 6, axis=-1)]
    fin_p = jnp.split(fin[:nbp], 2, axis=-1)
    fin_s = [jnp.repeat(m, dec_seq, axis=0) for m in jnp.split(fin[nbp:], 2, axis=-1)]

    w_main = w_in[0][:, :7 * width].astype(BF16)
    w_ab = jnp.pad(w_in[0][:, 7 * width:], ((0, 0), (0, LANES - 2 * n_gdn))).astype(BF16)
    g1 = norm1_g[0].reshape(1, d)
    g2 = norm2_g[0].reshape(1, d)
    conv_wt = conv_w[0].T
    w_oa = w_out[0][:width].astype(BF16)
    w_ob = w_out[0][width:].astype(BF16)
    w_r = jnp.pad(w_router[0], ((0, 0), (0, LANES - n_exp)))
    b_r = jnp.pad(b_router[0], (0, LANES - n_exp), constant_values=-1e30).reshape(1, LANES)
    ng = gdn_norm_g[0].reshape(1, LANES)

    def mixer(x, m, sba_fn, conv0, s0, n_seq, chunk):
        sh1, sc1, gt1, sh2, sc2, _ = m
        proj, gab = _inproj(x, g1, sh1, sc1, w_main, w_ab, width)
        o_sb = sba_fn(proj)
        o_gdn, s_new = _gdn(proj, gab, conv_wt, conv0, s0, a_log[0], dt_bias[0], ng, n_seq, chunk)
        x1, hffn, logits = _outproj(o_sb, o_gdn, x, w_oa, w_ob, gt1, g2, sh2, sc2, w_r, b_r)
        return proj, s_new, x1, hffn, logits[:, :n_exp]

    def conv_layout(c0):
        n = c0.shape[0]
        return c0.reshape(n, conv_taps - 1, 3, width).transpose(0, 2, 1, 3)

    xp = x_prompt.reshape(tp, d)
    xs = x_sample.reshape(ts, d)
    conv0_p = jnp.zeros((nbp, 3, conv_taps - 1, width), F32)
    s0_p = jnp.zeros((nbp, n_gdn, LANES, LANES), F32)
    proj_p, s_p, x1_p, h_p, lg_p = mixer(
        xp, mods_p, lambda pr: _sba_prompt(pr, sb_bias[0], n_sb), conv0_p, s0_p, nbp, GDN_CHUNK)
    k_pool = cache_k.reshape(cache_k.shape[1], cache_k.shape[2] * n_sb, LANES)
    v_pool = cache_v.reshape(cache_v.shape[1], cache_v.shape[2] * n_sb, LANES)
    proj_s, s_s, x1_s, h_s, lg_s = mixer(
        xs, mods_s, lambda pr: _sba_paged(pr, sb_bias[0], k_pool, v_pool, page_table, n_sb, dec_seq),
        conv_layout(state_conv[0]), state_gdn[0], nbs, dec_seq)

    hffn = jnp.concatenate([h_p, h_s], axis=0)
    logits = jnp.concatenate([lg_p, lg_s], axis=0)
    gates, dest, src_tok, block_e, n_active = _route(logits, n_exp)
    xb = hffn[src_tok]
    y = _moe_experts(xb, block_e, n_active, w_gu[0], b_gu[0], w_down[0], b_down[0])
    y_slots = y[dest.T.reshape(-1)].reshape(TOP_K, tp + ts, d)

    gf = norm_f_g.reshape(1, d)
    y_p = _final(x1_p, y_slots, gates[:tp], 0, mods_p[5], gf, fin_p[0], fin_p[1])
    y_s = _final(x1_s, y_slots, gates[tp:], tp, mods_s[5], gf, fin_s[0], fin_s[1])

    def new_conv(proj, n_seq):
        t = proj.shape[1]
        tail = proj[3:6].reshape(3, n_seq, t // n_seq, width)[:, :, -(conv_taps - 1):]
        return tail.transpose(1, 2, 0, 3).reshape(n_seq, conv_taps - 1, 3 * width)

    heads = lambda a, n, l: a.reshape(1, n, l, n_sb, LANES)
    return (y_p.reshape(nbp, seq, d), y_s.reshape(nbs, dec_seq, d),
            heads(proj_p[1], nbp, seq), heads(proj_p[2], nbp, seq),
            heads(proj_s[1], nbs, dec_seq), heads(proj_s[2], nbs, dec_seq),
            s_p[None], s_s[None], new_conv(proj_p, nbp)[None], new_conv(proj_s, nbs)[None])
```

```python
import functools

import jax
import jax.numpy as jnp
from jax import lax
from jax.experimental import pallas as pl
from jax.experimental.pallas import tpu as pltpu

F32 = jnp.float32
BF16 = jnp.bfloat16

LANES = 128
EPS = 1e-6
TOP_K = 4
SWIGLU_LIMIT = 7.0
SWIGLU_ALPHA = 1.702
MOE_BLOCK = 256
GDN_CHUNK = 64
SB_BLOCK = 256
SB_TILE_BLOCKS = 4
VMEM_LIMIT = 56 * 1024 * 1024
HIGHEST = lax.Precision.HIGHEST
LOG2E = 1.4426950408889634


def _cparams(sem):
    return pltpu.CompilerParams(dimension_semantics=sem, vmem_limit_bytes=VMEM_LIMIT)


def _silu(x):
    return x * jax.nn.sigmoid(x)


def _softplus(x):
    return jnp.maximum(x, 0.0) + jnp.log1p(jnp.exp(-jnp.abs(x)))


def _dot(a, b, precision=None):
    return jnp.dot(a, b, preferred_element_type=F32, precision=precision)


def _dot_nt(a, b, precision=None):
    return lax.dot_general(a, b, (((1,), (1,)), ((), ())), preferred_element_type=F32, precision=precision)


def _dot_tn(a, b, precision=None):
    return lax.dot_general(a, b, (((0,), (0,)), ((), ())), preferred_element_type=F32, precision=precision)


def _pick_tile(n, target):
    t = min(n, target)
    while n % t:
        t //= 2
    return t


def _ada_kernel(c_ref, w_ref, b_ref, o_ref):
    a = _silu(c_ref[...]).astype(BF16)
    o_ref[...] = _dot(a, w_ref[...].astype(BF16)) + b_ref[...]


def _ada(c_all, w, b):
    r, d = c_all.shape
    n = w.shape[1]
    tn = _pick_tile(n, 1024)
    return pl.pallas_call(
        _ada_kernel,
        grid=(n // tn,),
        in_specs=[pl.BlockSpec((r, d), lambda j: (0, 0)),
                  pl.BlockSpec((d, tn), lambda j: (0, j)),
                  pl.BlockSpec((1, tn), lambda j: (0, j))],
        out_specs=pl.BlockSpec((r, tn), lambda j: (0, j)),
        out_shape=jax.ShapeDtypeStruct((r, n), F32),
        compiler_params=_cparams(("arbitrary",)),
        name="ada_mod",
    )(c_all, w, b.reshape(1, n))


def _inproj_kernel(x_ref, g_ref, sh_ref, sc_ref, w_ref, wab_ref, proj_ref, gab_ref, k_ref, v_ref, h_scr):
    j = pl.program_id(1)

    @pl.when(j == 0)
    def _():
        x = x_ref[...]
        y = x * lax.rsqrt(jnp.mean(x * x, axis=-1, keepdims=True) + EPS) * g_ref[...]
        h = (y * (1.0 + sc_ref[...]) + sh_ref[...]).astype(BF16)
        h_scr[...] = h
        gab_ref[...] = _dot(h, wab_ref[...])

    res = _dot(h_scr[...], w_ref[...])
    proj_ref[0] = res

    tm, width = res.shape
    n_heads = width // LANES
    for plane, out_ref in ((1, k_ref), (2, v_ref)):
        @pl.when(j == plane)
        def _():
            for h in range(n_heads):
                out_ref[pl.ds(h, tm, stride=n_heads), :] = res[:, h * LANES:(h + 1) * LANES]


def _mod_spec(arr, tm):
    d = arr.shape[1]
    if arr.shape[0] == 1:
        return pl.BlockSpec((1, d), lambda i, *_: (0, 0))
    return pl.BlockSpec((tm, d), lambda i, *_: (i, 0))


def _inproj(x, g, sh, sc, w_main, w_ab, width):
    t, d = x.shape
    planes = w_main.shape[1] // width
    n_heads = width // LANES
    tm = _pick_tile(t, 512)
    return pl.pallas_call(
        _inproj_kernel,
        grid=(t // tm, planes),
        in_specs=[pl.BlockSpec((tm, d), lambda i, j: (i, 0)),
                  pl.BlockSpec((1, d), lambda i, j: (0, 0)),
                  _mod_spec(sh, tm), _mod_spec(sc, tm),
                  pl.BlockSpec((d, width), lambda i, j: (0, j)),
                  pl.BlockSpec((d, LANES), lambda i, j: (0, 0))],
        out_specs=[pl.BlockSpec((1, tm, width), lambda i, j: (j, i, 0)),
                   pl.BlockSpec((tm, LANES), lambda i, j: (i, 0)),
                   pl.BlockSpec((tm * n_heads, LANES), lambda i, j: (i, 0)),
                   pl.BlockSpec((tm * n_heads, LANES), lambda i, j: (i, 0))],
        out_shape=[jax.ShapeDtypeStruct((planes, t, width), F32),
                   jax.ShapeDtypeStruct((t, LANES), F32),
                   jax.ShapeDtypeStruct((t * n_heads, LANES), F32),
                   jax.ShapeDtypeStruct((t * n_heads, LANES), F32)],
        scratch_shapes=[pltpu.VMEM((tm, d), BF16)],
        compiler_params=_cparams(("arbitrary", "arbitrary")),
        name="in_proj",
    )(x, g, sh, sc, w_main, w_ab)


def _cum_matrix(blk):
    j = lax.broadcasted_iota(jnp.int32, (blk, blk + LANES), 0)
    s = lax.broadcasted_iota(jnp.int32, (blk, blk + LANES), 1)
    return jnp.where((j >= s) | (s >= blk), 1.0, 0.0).astype(BF16)


def _sb_block(zz, v, valid, cum_mat, blk):
    y = jnp.maximum(zz, 0.0) + jnp.log2(1.0 + jnp.exp2(-jnp.abs(zz)))
    if valid is not None:
        y = jnp.where(valid, y, 0.0)
    su = _dot(y.astype(BF16), cum_mat)
    p = jnp.exp2(zz - su[:, :blk])
    if valid is not None:
        p = jnp.where(valid, p, 0.0)
    return _dot(p.astype(BF16), v), su[:, blk:]


def _sba_prompt_kernel(bias_ref, q_ref, k_ref, v_ref, o_ref, q_scr, k_scr, v_scr, acc_scr, lsum_scr,
                       *, scale2, blk, ns):
    h = pl.program_id(0)
    qi = pl.program_id(1)
    tq = ns * blk

    @pl.when(qi == 0)
    def _():
        k_scr[...] = k_ref[0].astype(BF16)
        v_scr[...] = v_ref[0].astype(BF16)

    q_scr[...] = (q_ref[0] * scale2).astype(BF16)
    acc_scr[...] = jnp.zeros_like(acc_scr)
    lsum_scr[...] = jnp.zeros_like(lsum_scr)
    bias2 = bias_ref[h] * LOG2E
    cum_mat = _cum_matrix(blk)

    def visit(kb, row0, valid):
        start = pl.multiple_of(kb * blk, blk)
        zz = _dot_nt(q_scr[row0:tq, :], k_scr[pl.ds(start, blk), :]) + bias2
        pv, tot = _sb_block(zz, v_scr[pl.ds(start, blk), :], valid, cum_mat, blk)
        lsum = lsum_scr[row0:tq, :]
        acc_scr[row0:tq, :] += jnp.exp2(-lsum) * pv
        lsum_scr[row0:tq, :] = lsum + tot

    for j in reversed(range(ns)):
        row = lax.broadcasted_iota(jnp.int32, (tq - j * blk, blk), 0)
        col = lax.broadcasted_iota(jnp.int32, (tq - j * blk, blk), 1)
        visit(qi * ns + j, j * blk, col < row)

    step = 2 if ns % 2 == 0 else 1

    def earlier(t, carry):
        for u in range(step):
            visit(qi * ns - 1 - (t * step + u), 0, None)
        return carry

    lax.fori_loop(0, qi * (ns // step), earlier, 0)
    o_ref[...] = acc_scr[...]


def _sba_prompt(proj, bias, n_heads):
    _, l, w = proj.shape
    blk = SB_BLOCK
    ns = _pick_tile(l // blk, SB_TILE_BLOCKS)
    tq = ns * blk
    kern = functools.partial(_sba_prompt_kernel, scale2=LANES ** -0.5 * LOG2E, blk=blk, ns=ns)
    return pl.pallas_call(
        kern,
        grid=(n_heads, l // tq),
        in_specs=[pl.BlockSpec(memory_space=pltpu.SMEM),
                  pl.BlockSpec((1, tq, LANES), lambda h, i: (0, i, h)),
                  pl.BlockSpec((1, l, LANES), lambda h, i: (1, 0, h)),
                  pl.BlockSpec((1, l, LANES), lambda h, i: (2, 0, h))],
        out_specs=pl.BlockSpec((tq, LANES), lambda h, i: (i, h)),
        out_shape=jax.ShapeDtypeStruct((l, w), F32),
        scratch_shapes=[pltpu.VMEM((tq, LANES), BF16), pltpu.VMEM((l, LANES), BF16), pltpu.VMEM((l, LANES), BF16),
                        pltpu.VMEM((tq, LANES), F32), pltpu.VMEM((tq, LANES), F32)],
        compiler_params=_cparams(("arbitrary", "arbitrary")),
        name="sba_prompt",
    )(bias, proj, proj, proj)


def _sba_paged_kernel(pt_ref, bias_ref, q_ref, kn_ref, vn_ref, *rest, n_heads, s, scale2, group):
    kp_refs = rest[:group]
    vp_refs = rest[group:2 * group]
    o_ref, wq_scr, bias_scr, acc_scr, lsum_scr = rest[2 * group:]
    p = pl.program_id(1)
    page = kp_refs[0].shape[1] // n_heads
    hq = n_heads * s
    w = n_heads * LANES
    cum_mat = _cum_matrix(page)

    def visit(ks, vs, valid):
        n = len(ks)
        zz = _dot_nt(wq_scr[...], jnp.concatenate(ks, axis=0))
        zz = jnp.concatenate([zz[:, g * page:(g + 1) * page] for g in range(n)], axis=0) + \
            jnp.concatenate([bias_scr[...]] * n, axis=0)
        y = jnp.maximum(zz, 0.0) + jnp.log2(1.0 + jnp.exp2(-jnp.abs(zz)))
        if valid is not None:
            y = jnp.where(valid, y, 0.0)
        su = _dot(y.astype(BF16), cum_mat)
        p = jnp.exp2(zz - su[:, :page])
        if valid is not None:
            p = jnp.where(valid, p, 0.0)
        p = p.astype(BF16)
        acc = acc_scr[...]
        lsum = lsum_scr[...]
        for g in range(n):
            pv = _dot(p[g * hq:(g + 1) * hq], vs[g])
            pv = jnp.concatenate([pv[h * s:(h + 1) * s, h * LANES:(h + 1) * LANES] for h in range(n_heads)], axis=0)
            acc = acc + jnp.exp2(-lsum) * pv
            lsum = lsum + su[g * hq:(g + 1) * hq, page:]
        acc_scr[...] = acc
        lsum_scr[...] = lsum

    @pl.when(p == 0)
    def _():
        qt = jnp.concatenate([q_ref[0] * scale2] * n_heads, axis=0)
        r2 = lax.broadcasted_iota(jnp.int32, (hq, w), 0)
        c2 = lax.broadcasted_iota(jnp.int32, (hq, w), 1)
        wq_scr[...] = jnp.where(r2 // s == c2 // LANES, qt, 0.0).astype(BF16)
        row = lax.broadcasted_iota(jnp.int32, (hq, page), 0)
        col = lax.broadcasted_iota(jnp.int32, (hq, page), 1)
        bias = jnp.zeros((hq, page), F32)
        for h in range(n_heads):
            bias = jnp.where(row // s == h, bias_ref[h] * LOG2E, bias)
        bias_scr[...] = bias
        acc_scr[...] = jnp.zeros_like(acc_scr)
        lsum_scr[...] = jnp.zeros_like(lsum_scr)
        pad = jnp.zeros((page - s, w), F32)
        kn = jnp.concatenate([kn_ref[0], pad], axis=0).astype(BF16)
        vn = jnp.concatenate([vn_ref[0], pad], axis=0).astype(BF16)
        visit([kn], [vn], col < row % s)

    def head_major(ref):
        return jnp.concatenate([ref[0, pl.ds(h, page, stride=n_heads), :] for h in range(n_heads)],
                               axis=1).astype(BF16)

    visit([head_major(r) for r in kp_refs], [head_major(r) for r in vp_refs], None)

    @pl.when(p == pl.num_programs(1) - 1)
    def _():
        for h in range(n_heads):
            o_ref[:, h * LANES:(h + 1) * LANES] = acc_scr[h * s:(h + 1) * s, :]


def _sba_paged(proj, bias, k_pool, v_pool, page_table, n_heads, s):
    _, t, w = proj.shape
    nb, n_pages = page_table.shape
    rows = k_pool.shape[1]
    page = rows // n_heads
    hq = n_heads * s
    group = _pick_tile(n_pages, 8)
    kern = functools.partial(_sba_paged_kernel, n_heads=n_heads, s=s, scale2=LANES ** -0.5 * LOG2E, group=group)

    def pool_spec(g):
        return pl.BlockSpec((1, rows, LANES), lambda b, p, pt: (pt[b, n_pages - 1 - (p * group + g)], 0, 0))

    pool_specs = [pool_spec(g) for g in range(group)]
    return pl.pallas_call(
        kern,
        grid_spec=pltpu.PrefetchScalarGridSpec(
            num_scalar_prefetch=1,
            grid=(nb, n_pages // group),
            in_specs=[pl.BlockSpec(memory_space=pltpu.SMEM),
                      pl.BlockSpec((1, s, w), lambda b, p, pt: (0, b, 0)),
                      pl.BlockSpec((1, s, w), lambda b, p, pt: (1, b, 0)),
                      pl.BlockSpec((1, s, w), lambda b, p, pt: (2, b, 0))] + pool_specs + pool_specs,
            out_specs=pl.BlockSpec((s, w), lambda b, p, pt: (b, 0)),
            scratch_shapes=[pltpu.VMEM((hq, w), BF16), pltpu.VMEM((hq, page), F32),
                            pltpu.VMEM((hq, LANES), F32), pltpu.VMEM((hq, page), F32)]),
        out_shape=jax.ShapeDtypeStruct((t, w), F32),
        compiler_params=_cparams(("arbitrary", "arbitrary")),
        name="sba_paged",
    )(page_table, bias, proj, proj, proj, *([k_pool] * group), *([v_pool] * group))


_NN = ((1,), (0,))
_NT = ((1,), (1,))


def _split(a, parts):
    out = []
    for _ in range(parts - 1):
        hi = a.astype(BF16)
        out.append(hi)
        a = a - hi.astype(F32)
    out.append(a.astype(BF16))
    return out


def _mm(a, b, dims, split):
    if not split:
        return lax.dot_general(a, b, (dims, ((), ())), preferred_element_type=F32, precision=HIGHEST)
    d = lambda x, y: lax.dot_general(x, y, (dims, ((), ())), preferred_element_type=F32)
    return d(a[1], b[0]) + d(a[0], b[1]) + d(a[0], b[0])


def _gdn_chunk(q, k, v, g_col, beta, state, c):
    heads = range(len(q))
    split = c >= 16
    sp = (lambda x: _split(x, 2)) if split else (lambda x: x)
    ri = lax.broadcasted_iota(jnp.int32, (c, c), 0)
    cj = lax.broadcasted_iota(jnp.int32, (c, c), 1)
    eye = (ri == cj).astype(F32)
    g_row = [jnp.sum(jnp.where(ri == cj, g_col[h], 0.0), axis=0, keepdims=True) for h in heads]
    if split:
        incl = (cj <= ri).astype(BF16)
        incl_t = (ri <= cj).astype(BF16)
        gc_col = [sum(_dot(incl, part) for part in reversed(_split(jnp.broadcast_to(g_col[h], (c, LANES)), 3)))
                  for h in heads]
        gc_row = [sum(_dot(part, incl_t) for part in reversed(_split(jnp.broadcast_to(g_row[h], (16, c)), 3)))[0:1]
                  for h in heads]
    else:
        gc_col = [_dot((cj <= ri).astype(F32), jnp.broadcast_to(g_col[h], (c, LANES)), HIGHEST) for h in heads]
        gc_row = [_dot(jnp.broadcast_to(g_row[h], (8, c)), (ri <= cj).astype(F32), HIGHEST)[0:1] for h in heads]
    decay = [jnp.where(cj <= ri, jnp.exp(gc_col[h][:, 0:1] - gc_row[h]), 0.0) for h in heads]
    kb = [k[h] * beta[h] for h in heads]
    k_s = [sp(k[h]) for h in heads]
    kk = [_mm(sp(kb[h]), k_s[h], _NT, split) for h in heads]
    qk = [_mm(sp(q[h]), k_s[h], _NT, split) * decay[h] for h in heads]
    n = [jnp.where(cj < ri, -kk[h] * decay[h], 0.0) for h in heads]
    inv = [eye + n[h] for h in heads]
    i = 2
    while i < c:
        n_s = [sp(n[h]) for h in heads]
        n = [_mm(n_s[h], n_s[h], _NN, split) for h in heads]
        inv = [_mm(sp(inv[h]), sp(eye + n[h]), _NN, split) for h in heads]
        i *= 2
    rhs = [jnp.concatenate([v[h] * beta[h], kb[h] * jnp.exp(gc_col[h])], axis=1) for h in heads]
    sol = [_mm(sp(inv[h]), sp(rhs[h]), _NN, split) for h in heads]
    state_s = [sp(state[h]) for h in heads]
    ws = [_mm(sp(sol[h][:, LANES:]), state_s[h], _NN, split) for h in heads]
    qs = [_mm(sp(q[h] * jnp.exp(gc_col[h])), state_s[h], _NN, split) for h in heads]
    v_new = [sol[h][:, :LANES] - ws[h] for h in heads]
    g_last = [gc_col[h][c - 1:c, :] for h in heads]
    k_dec = [k[h] * jnp.exp(g_last[h] - gc_col[h]) for h in heads]
    o = [qs[h] + _mm(sp(qk[h]), sp(v_new[h]), _NN, split) for h in heads]
    new_state = [state[h] * jnp.exp(g_last[h]) + _dot_tn(k_dec[h], v_new[h], HIGHEST) for h in heads]
    return o, new_state


def _gdn_kernel(alog_ref, dtb_ref, q_ref, k_ref, v_ref, z_ref, gab_ref, cw_ref, c0_ref, s0_ref, ng_ref,
                o_ref, s_ref, tail_scr, state_scr, *, c, n_heads, conv_w):
    ci = pl.program_id(1)
    halo = 8
    w = n_heads * LANES

    @pl.when(ci == 0)
    def _():
        state_scr[...] = s0_ref[0]
        for a in range(3):
            tail_scr[a, 0:halo, :] = jnp.zeros((halo, w), F32)
            tail_scr[a, halo - (conv_w - 1):halo, :] = c0_ref[0, a]

    def conv(a, x_ref):
        tail_scr[a, halo:halo + c, :] = x_ref[0]
        lo = halo - (conv_w - 1)
        y = sum(tail_scr[a, lo + i:lo + i + c, :] * cw_ref[i:i + 1, a * w:(a + 1) * w] for i in range(conv_w))
        tail_scr[a, 0:halo, :] = tail_scr[a, c:c + halo, :]
        return _silu(y)

    q_all = conv(0, q_ref)
    k_all = conv(1, k_ref)
    v_all = conv(2, v_ref)
    gab = gab_ref[...]
    lane = lax.broadcasted_iota(jnp.int32, gab.shape, 1)

    heads = range(n_heads)
    sl = [slice(h * LANES, (h + 1) * LANES) for h in heads]
    q = [q_all[:, sl[h]] for h in heads]
    k = [k_all[:, sl[h]] for h in heads]
    q = [q[h] * lax.rsqrt(jnp.sum(q[h] * q[h], axis=-1, keepdims=True) + EPS) * (LANES ** -0.5) for h in heads]
    k = [k[h] * lax.rsqrt(jnp.sum(k[h] * k[h], axis=-1, keepdims=True) + EPS) for h in heads]
    ga = [jnp.sum(jnp.where(lane == h, gab, 0.0), axis=-1, keepdims=True) for h in heads]
    gb = [jnp.sum(jnp.where(lane == h + n_heads, gab, 0.0), axis=-1, keepdims=True) for h in heads]
    g_col = [-jnp.exp(alog_ref[h]) * _softplus(ga[h] + dtb_ref[h]) for h in heads]
    beta = [jax.nn.sigmoid(gb[h]) for h in heads]
    o, state = _gdn_chunk(q, k, [v_all[:, sl[h]] for h in heads], g_col, beta, [state_scr[h] for h in heads], c)
    for h in heads:
        state_scr[h] = state[h]
        on = o[h] * lax.rsqrt(jnp.mean(o[h] * o[h], axis=-1, keepdims=True) + EPS) * ng_ref[...]
        o_ref[:, sl[h]] = on * _silu(z_ref[0, :, sl[h]])

    @pl.when(ci == pl.num_programs(1) - 1)
    def _():
        s_ref[0] = state_scr[...]


def _gdn(proj, gab, conv_wt, conv0, s0, a_log, dt_bias, norm_g, n_seq, c):
    _, t, w = proj.shape
    n_heads = w // LANES
    l = t // n_seq
    nc = l // c
    conv_w = conv_wt.shape[0]
    kern = functools.partial(_gdn_kernel, c=c, n_heads=n_heads, conv_w=conv_w)

    def tok(plane):
        return pl.BlockSpec((1, c, w), lambda n, ci: (plane, n * nc + ci, 0))

    smem = pl.BlockSpec(memory_space=pltpu.SMEM)
    return pl.pallas_call(
        kern,
        grid=(n_seq, nc),
        in_specs=[smem, smem, tok(3), tok(4), tok(5), tok(6),
                  pl.BlockSpec((c, LANES), lambda n, ci: (n * nc + ci, 0)),
                  pl.BlockSpec((conv_w, 3 * w), lambda n, ci: (0, 0)),
                  pl.BlockSpec((1, 3, conv_w - 1, w), lambda n, ci: (n, 0, 0, 0)),
                  pl.BlockSpec((1, n_heads, LANES, LANES), lambda n, ci: (n, 0, 0, 0)),
                  pl.BlockSpec((1, LANES), lambda n, ci: (0, 0))],
        out_specs=[pl.BlockSpec((c, w), lambda n, ci: (n * nc + ci, 0)),
                   pl.BlockSpec((1, n_heads, LANES, LANES), lambda n, ci: (n, 0, 0, 0))],
        out_shape=[jax.ShapeDtypeStruct((t, w), F32),
                   jax.ShapeDtypeStruct((n_seq, n_heads, LANES, LANES), F32)],
        scratch_shapes=[pltpu.VMEM((3, c + 8, w), F32), pltpu.VMEM((n_heads, LANES, LANES), F32)],
        compiler_params=_cparams(("arbitrary", "arbitrary")),
        name="gdn",
    )(a_log, dt_bias, proj, proj, proj, proj, gab, conv_wt, conv0, s0, norm_g)


def _outproj_kernel(osb_ref, ogdn_ref, x_ref, wa_ref, wb_ref, gt_ref, g_ref, sh_ref, sc_ref, wr_ref, br_ref,
                    x1_ref, h_ref, lg_ref):
    mix = _dot(osb_ref[...].astype(BF16), wa_ref[...]) + _dot(ogdn_ref[...].astype(BF16), wb_ref[...])
    x1 = x_ref[...] + gt_ref[...] * mix
    x1_ref[...] = x1
    y = x1 * lax.rsqrt(jnp.mean(x1 * x1, axis=-1, keepdims=True) + EPS) * g_ref[...]
    hf = y * (1.0 + sc_ref[...]) + sh_ref[...]
    h_ref[...] = hf
    lg_ref[...] = _dot(hf, wr_ref[...], HIGHEST) + br_ref[...]


def _outproj(o_sb, o_gdn, x, w_a, w_b, gt, g, sh, sc, w_r, b_r):
    t, d = x.shape
    w = o_sb.shape[1]
    tm = _pick_tile(t, 256)
    row = lambda i: (i, 0)
    fixed = lambda i: (0, 0)
    return pl.pallas_call(
        _outproj_kernel,
        grid=(t // tm,),
        in_specs=[pl.BlockSpec((tm, w), row), pl.BlockSpec((tm, w), row), pl.BlockSpec((tm, d), row),
                  pl.BlockSpec((w, d), fixed), pl.BlockSpec((w, d), fixed),
                  _mod_spec(gt, tm), pl.BlockSpec((1, d), fixed), _mod_spec(sh, tm), _mod_spec(sc, tm),
                  pl.BlockSpec((d, LANES), fixed), pl.BlockSpec((1, LANES), fixed)],
        out_specs=[pl.BlockSpec((tm, d), row), pl.BlockSpec((tm, d), row), pl.BlockSpec((tm, LANES), row)],
        out_shape=[jax.ShapeDtypeStruct((t, d), F32), jax.ShapeDtypeStruct((t, d), F32),
                   jax.ShapeDtypeStruct((t, LANES), F32)],
        compiler_params=_cparams(("arbitrary",)),
        name="out_proj",
    )(o_sb, o_gdn, x, w_a, w_b, gt, g, sh, sc, w_r, b_r)


def _expert_weights(be_ref, nx_ref, ri_ref, meta_ref, w_hbm, wbuf, w_scr, sem, col_tiles, tile):
    j = pl.program_id(0)
    i = pl.program_id(1)
    n_sweeps = pl.num_programs(0)
    active = i < meta_ref[0]
    n_runs = meta_ref[1]
    fresh = jnp.logical_and(active, jnp.logical_or(i == 0, be_ref[i] != be_ref[jnp.maximum(i - 1, 0)]))

    def copies(e, jj, slot):
        return [pltpu.make_async_copy(w_hbm.at[e, :, pl.ds(pl.multiple_of((c0 + jj) * tile, tile), tile)],
                                      wbuf.at[slot, a], sem.at[slot, a]) for a, c0 in enumerate(col_tiles)]

    @pl.when(jnp.logical_and(j == 0, i == 0))
    def _():
        for cp in copies(be_ref[0], 0, 0):
            cp.start()

    @pl.when(fresh)
    def _():
        slot = (j * n_runs + ri_ref[i]) % 2
        for cp in copies(be_ref[i], j, slot):
            cp.wait()
        nxt = nx_ref[i]
        more_runs = nxt >= 0

        @pl.when(jnp.logical_or(more_runs, j + 1 < n_sweeps))
        def _():
            for cp in copies(jnp.where(more_runs, nxt, be_ref[0]), jnp.where(more_runs, j, j + 1), 1 - slot):
                cp.start()

        for a in range(len(col_tiles)):
            w_scr[a] = wbuf[slot, a].astype(BF16)

    return active


def _moe_up_kernel(be_ref, nx_ref, ri_ref, meta_ref, x_ref, w_hbm, bg_ref, bu_ref, h_ref, wbuf, w_scr, sem, *, nf):
    tf = h_ref.shape[1]
    active = _expert_weights(be_ref, nx_ref, ri_ref, meta_ref, w_hbm, wbuf, w_scr, sem, (0, nf), tf)

    @pl.when(active)
    def _():
        x = x_ref[...].astype(BF16)
        gate = jnp.minimum(_dot(x, w_scr[0]) + bg_ref[0], SWIGLU_LIMIT)
        up = jnp.clip(_dot(x, w_scr[1]) + bu_ref[0], -SWIGLU_LIMIT, SWIGLU_LIMIT)
        h_ref[...] = ((up + 1.0) * gate * jax.nn.sigmoid(gate * SWIGLU_ALPHA)).astype(BF16)


def _moe_down_kernel(be_ref, nx_ref, ri_ref, meta_ref, h_ref, w_hbm, bd_ref, y_ref, wbuf, w_scr, sem):
    tn = y_ref.shape[1]
    active = _expert_weights(be_ref, nx_ref, ri_ref, meta_ref, w_hbm, wbuf, w_scr, sem, (0,), tn)

    @pl.when(active)
    def _():
        y_ref[...] = _dot(h_ref[...], w_scr[0]) + bd_ref[0]


def _moe_experts(xb, block_e, n_active, counts, w_gu, b_gu, w_down, b_down):
    cap, d = xb.shape
    n_blocks = cap // MOE_BLOCK
    n_exp, _, ff2 = w_gu.shape
    ff = ff2 // 2
    tf = _pick_tile(ff, 1024)
    nf = ff // tf

    has = counts > 0
    ids = jnp.arange(n_exp, dtype=jnp.int32)
    later = jnp.where((ids[None, :] > ids[:, None]) & has[None, :], ids[None, :], n_exp)
    next_nonempty = jnp.min(later, axis=1)
    next_nonempty = jnp.where(next_nonempty == n_exp, -1, next_nonempty).astype(jnp.int32)
    run_of_expert = (jnp.cumsum(has.astype(jnp.int32)) - has.astype(jnp.int32)).astype(jnp.int32)
    next_e = next_nonempty[block_e]
    run_idx = run_of_expert[block_e]
    meta = jnp.concatenate([n_active, jnp.sum(has.astype(jnp.int32)).reshape(1)]).astype(jnp.int32)

    def blk(i, meta):
        return jnp.minimum(i, meta[0] - 1)

    hbm = pl.BlockSpec(memory_space=pl.ANY)
    h = pl.pallas_call(
        functools.partial(_moe_up_kernel, nf=nf),
        grid_spec=pltpu.PrefetchScalarGridSpec(
            num_scalar_prefetch=4,
            grid=(nf, n_blocks),
            in_specs=[pl.BlockSpec((MOE_BLOCK, d), lambda j, i, be, nx, ri, meta: (blk(i, meta), 0)),
                      hbm,
                      pl.BlockSpec((1, 1, tf), lambda j, i, be, nx, ri, meta: (be[blk(i, meta)], 0, j)),
                      pl.BlockSpec((1, 1, tf), lambda j, i, be, nx, ri, meta: (be[blk(i, meta)], 0, nf + j))],
            out_specs=pl.BlockSpec((MOE_BLOCK, tf), lambda j, i, be, nx, ri, meta: (blk(i, meta), j)),
            scratch_shapes=[pltpu.VMEM((2, 2, d, tf), F32), pltpu.VMEM((2, d, tf), BF16),
                            pltpu.SemaphoreType.DMA((2, 2))]),
        out_shape=jax.ShapeDtypeStruct((cap, ff), BF16),
        compiler_params=_cparams(("arbitrary", "arbitrary")),
        name="moe_up",
    )(block_e, next_e, run_idx, meta, xb, w_gu, b_gu.reshape(n_exp, 1, ff2), b_gu.reshape(n_exp, 1, ff2))

    tn = _pick_tile(d, 2048)
    return pl.pallas_call(
        _moe_down_kernel,
        grid_spec=pltpu.PrefetchScalarGridSpec(
            num_scalar_prefetch=4,
            grid=(d // tn, n_blocks),
            in_specs=[pl.BlockSpec((MOE_BLOCK, ff), lambda j, i, be, nx, ri, meta: (blk(i, meta), 0)),
                      hbm,
                      pl.BlockSpec((1, 1, tn), lambda j, i, be, nx, ri, meta: (be[blk(i, meta)], 0, j))],
            out_specs=pl.BlockSpec((MOE_BLOCK, tn), lambda j, i, be, nx, ri, meta: (blk(i, meta), j)),
            scratch_shapes=[pltpu.VMEM((2, 1, ff, tn), F32), pltpu.VMEM((1, ff, tn), BF16),
                            pltpu.SemaphoreType.DMA((2, 1))]),
        out_shape=jax.ShapeDtypeStruct((cap, d), F32),
        compiler_params=_cparams(("arbitrary", "arbitrary")),
        name="moe_down",
    )(block_e, next_e, run_idx, meta, h, w_down, b_down.reshape(n_exp, 1, d))


def _route(logits, n_exp):
    t = logits.shape[0]
    top_logit, top_idx = lax.top_k(logits, TOP_K)
    gates = jax.nn.softmax(top_logit, axis=-1)
    n_slots = t * TOP_K
    flat_e = top_idx.reshape(-1)
    onehot = (flat_e[:, None] == jnp.arange(n_exp)[None, :]).astype(jnp.int32)
    rank = jnp.take_along_axis(jnp.cumsum(onehot, axis=0) - onehot, flat_e[:, None], axis=1)[:, 0]
    counts = jnp.sum(onehot, axis=0)
    padded = (counts + MOE_BLOCK - 1) // MOE_BLOCK * MOE_BLOCK
    pad_end = jnp.cumsum(padded)
    pad_start = pad_end - padded
    dest = (pad_start[flat_e] + rank).astype(jnp.int32)
    n_blocks = -(-n_slots // MOE_BLOCK) + n_exp
    cap = n_blocks * MOE_BLOCK
    src_tok = jnp.zeros((cap,), jnp.int32).at[dest].set(jnp.arange(n_slots, dtype=jnp.int32) // TOP_K)
    block_start = jnp.arange(n_blocks, dtype=jnp.int32) * MOE_BLOCK
    block_e = jnp.minimum(jnp.sum((block_start[:, None] >= pad_end[None, :]).astype(jnp.int32), axis=1),
                          n_exp - 1).astype(jnp.int32)
    n_active = (pad_end[-1] // MOE_BLOCK).astype(jnp.int32).reshape(1)
    return gates, dest.reshape(t, TOP_K), src_tok, block_e, n_active, counts


def _final_kernel(x_ref, y_ref, gate_ref, gt_ref, g_ref, sh_ref, sc_ref, o_ref):
    gates = gate_ref[...]
    ffn = sum(y_ref[k] * gates[:, k:k + 1] for k in range(TOP_K))
    x = x_ref[...] + gt_ref[...] * ffn
    y = x * lax.rsqrt(jnp.mean(x * x, axis=-1, keepdims=True) + EPS) * g_ref[...]
    o_ref[...] = y * (1.0 + sc_ref[...]) + sh_ref[...]


def _final(x1, y_slots, gates, row0, gt, g, sh, sc):
    t, d = x1.shape
    tm = _pick_tile(t, 256)
    assert row0 % tm == 0
    row = lambda i: (i, 0)
    return pl.pallas_call(
        _final_kernel,
        grid=(t // tm,),
        in_specs=[pl.BlockSpec((tm, d), row),
                  pl.BlockSpec((TOP_K, tm, d), lambda i: (0, row0 // tm + i, 0)),
                  pl.BlockSpec((tm, TOP_K), row), _mod_spec(gt, tm),
                  pl.BlockSpec((1, d), lambda i: (0, 0)), _mod_spec(sh, tm), _mod_spec(sc, tm)],
        out_specs=pl.BlockSpec((tm, d), row),
        out_shape=jax.ShapeDtypeStruct((t, d), F32),
        compiler_params=_cparams(("arbitrary",)),
        name="final_norm",
    )(x1, y_slots, gates, gt, g, sh, sc)


def kernel(x_prompt, x_sample, cache_k, cache_v, state_gdn, state_conv, page_table, c_prompt, c_sample, w_ada, b_ada, norm1_g, norm2_g, w_in, sb_bias, conv_w, a_log, dt_bias, gdn_norm_g, w_out, w_router, b_router, w_gu, b_gu, w_down, b_down, w_ada_final, b_ada_final, norm_f_g):
    depth = w_ada.shape[0]
    assert depth == 1, "single-layer stack"
    nbp, seq, d = x_prompt.shape
    nbs, dec_seq, _ = x_sample.shape
    assert nbp == 1
    n_sb = cache_k.shape[-2]
    n_gdn = state_gdn.shape[2]
    width = n_sb * LANES
    assert cache_k.shape[-1] == LANES and n_gdn * LANES == width
    conv_taps = conv_w.shape[-1]
    n_exp = w_router.shape[-1]
    tp, ts = nbp * seq, nbs * dec_seq

    c_all = jnp.concatenate([c_prompt, c_sample], axis=0)
    n_c = c_all.shape[0]
    c_all = jnp.pad(c_all, ((0, -n_c % 8), (0, 0)))
    mods = _ada(c_all, w_ada[0], b_ada[0])[:n_c]
    fin = _ada(c_all, w_ada_final, b_ada_final)[:n_c]
    mods_p = [m for m in jnp.split(mods[:nbp], 6, axis=-1)]
    mods_s = [jnp.repeat(m, dec_seq, axis=0) for m in jnp.split(mods[nbp:], 6, axis=-1)]
    fin_p = jnp.split(fin[:nbp], 2, axis=-1)
    fin_s = [jnp.repeat(m, dec_seq, axis=0) for m in jnp.split(fin[nbp:], 2, axis=-1)]

    w_main = w_in[0][:, :7 * width].astype(BF16)
    w_ab = jnp.pad(w_in[0][:, 7 * width:], ((0, 0), (0, LANES - 2 * n_gdn))).astype(BF16)
    g1 = norm1_g[0].reshape(1, d)
    g2 = norm2_g[0].reshape(1, d)
    conv_wt = conv_w[0].T
    w_oa = w_out[0][:width].astype(BF16)
    w_ob = w_out[0][width:].astype(BF16)
    w_r = jnp.pad(w_router[0], ((0, 0), (0, LANES - n_exp)))
    b_r = jnp.pad(b_router[0], (0, LANES - n_exp), constant_values=-1e30).reshape(1, LANES)
    ng = gdn_norm_g[0].reshape(1, LANES)

    def mixer(x, m, sba_fn, conv0, s0, n_seq, chunk):
        sh1, sc1, gt1, sh2, sc2, _ = m
        proj, gab, k_new, v_new = _inproj(x, g1, sh1, sc1, w_main, w_ab, width)
        o_sb = sba_fn(proj)
        o_gdn, s_new = _gdn(proj, gab, conv_wt, conv0, s0, a_log[0], dt_bias[0], ng, n_seq, chunk)
        x1, hffn, logits = _outproj(o_sb, o_gdn, x, w_oa, w_ob, gt1, g2, sh2, sc2, w_r, b_r)
        return (proj, k_new, v_new), s_new, x1, hffn, logits[:, :n_exp]

    def conv_layout(c0):
        n = c0.shape[0]
        return c0.reshape(n, conv_taps - 1, 3, width).transpose(0, 2, 1, 3)

    xp = x_prompt.reshape(tp, d)
    xs = x_sample.reshape(ts, d)
    conv0_p = jnp.zeros((nbp, 3, conv_taps - 1, width), F32)
    s0_p = jnp.zeros((nbp, n_gdn, LANES, LANES), F32)
    kv_p, s_p, x1_p, h_p, lg_p = mixer(
        xp, mods_p, lambda pr: _sba_prompt(pr, sb_bias[0], n_sb), conv0_p, s0_p, nbp, GDN_CHUNK)
    k_pool = cache_k.reshape(cache_k.shape[1], cache_k.shape[2] * n_sb, LANES)
    v_pool = cache_v.reshape(cache_v.shape[1], cache_v.shape[2] * n_sb, LANES)
    kv_s, s_s, x1_s, h_s, lg_s = mixer(
        xs, mods_s, lambda pr: _sba_paged(pr, sb_bias[0], k_pool, v_pool, page_table, n_sb, dec_seq),
        conv_layout(state_conv[0]), state_gdn[0], nbs, dec_seq)

    hffn = jnp.concatenate([h_p, h_s], axis=0)
    logits = jnp.concatenate([lg_p, lg_s], axis=0)
    gates, dest, src_tok, block_e, n_active, counts = _route(logits, n_exp)
    xb = hffn[src_tok]
    y = _moe_experts(xb, block_e, n_active, counts, w_gu[0], b_gu[0], w_down[0], b_down[0])
    y_slots = y[dest.T.reshape(-1)].reshape(TOP_K, tp + ts, d)

    gf = norm_f_g.reshape(1, d)
    y_p = _final(x1_p, y_slots, gates[:tp], 0, mods_p[5], gf, fin_p[0], fin_p[1])
    y_s = _final(x1_s, y_slots, gates[tp:], tp, mods_s[5], gf, fin_s[0], fin_s[1])

    def new_conv(proj, n_seq):
        planes, t, _ = proj.shape
        l = t // n_seq
        tail = lax.slice(proj.reshape(planes, n_seq, l, width), (3, 0, l - (conv_taps - 1), 0), (6, n_seq, l, width))
        return tail.transpose(1, 2, 0, 3).reshape(n_seq, conv_taps - 1, 3 * width)

    heads = lambda a, n, l: a.reshape(1, n, l, n_sb, LANES)
    return (y_p.reshape(nbp, seq, d), y_s.reshape(nbs, dec_seq, d),
            heads(kv_p[1], nbp, seq), heads(kv_p[2], nbp, seq),
            heads(kv_s[1], nbs, dec_seq), heads(kv_s[2], nbs, dec_seq),
            s_p[None], s_s[None], new_conv(kv_p[0], nbp)[None], new_conv(kv_s[0], nbs)[None])
```

```python
import functools

import jax
import jax.numpy as jnp
from jax import lax
from jax.experimental import pallas as pl
from jax.experimental.pallas import tpu as pltpu

F32 = jnp.float32
BF16 = jnp.bfloat16

LANES = 128
EPS = 1e-6
TOP_K = 4
SWIGLU_LIMIT = 7.0
SWIGLU_ALPHA = 1.702
MOE_BLOCK = 256
GDN_CHUNK = 64
SB_BLOCK = 256
SB_TILE_BLOCKS = 4
VMEM_LIMIT = 56 * 1024 * 1024
HIGHEST = lax.Precision.HIGHEST
LOG2E = 1.4426950408889634


def _cparams(sem):
    return pltpu.CompilerParams(dimension_semantics=sem, vmem_limit_bytes=VMEM_LIMIT)


def _silu(x):
    return x * jax.nn.sigmoid(x)


def _softplus(x):
    return jnp.maximum(x, 0.0) + jnp.log1p(jnp.exp(-jnp.abs(x)))


def _dot(a, b, precision=None):
    return jnp.dot(a, b, preferred_element_type=F32, precision=precision)


def _dot_nt(a, b, precision=None):
    return lax.dot_general(a, b, (((1,), (1,)), ((), ())), preferred_element_type=F32, precision=precision)


def _dot_tn(a, b, precision=None):
    return lax.dot_general(a, b, (((0,), (0,)), ((), ())), preferred_element_type=F32, precision=precision)


def _pick_tile(n, target):
    t = min(n, target)
    while n % t:
        t //= 2
    return t


def _ada_kernel(c_ref, w_ref, b_ref, o_ref):
    a = _silu(c_ref[...]).astype(BF16)
    o_ref[...] = _dot(a, w_ref[...].astype(BF16)) + b_ref[...]


def _ada(c_all, w, b):
    r, d = c_all.shape
    n = w.shape[1]
    tn = _pick_tile(n, 1024)
    return pl.pallas_call(
        _ada_kernel,
        grid=(n // tn,),
        in_specs=[pl.BlockSpec((r, d), lambda j: (0, 0)),
                  pl.BlockSpec((d, tn), lambda j: (0, j)),
                  pl.BlockSpec((1, tn), lambda j: (0, j))],
        out_specs=pl.BlockSpec((r, tn), lambda j: (0, j)),
        out_shape=jax.ShapeDtypeStruct((r, n), F32),
        compiler_params=_cparams(("arbitrary",)),
        name="ada_mod",
    )(c_all, w, b.reshape(1, n))


def _inproj_kernel(x_ref, g_ref, sh_ref, sc_ref, w_ref, wab_ref, proj_ref, gab_ref, k_ref, v_ref, h_scr):
    j = pl.program_id(1)

    @pl.when(j == 0)
    def _():
        x = x_ref[...]
        y = x * lax.rsqrt(jnp.mean(x * x, axis=-1, keepdims=True) + EPS) * g_ref[...]
        h = (y * (1.0 + sc_ref[...]) + sh_ref[...]).astype(BF16)
        h_scr[...] = h
        gab_ref[...] = _dot(h, wab_ref[...])

    res = _dot(h_scr[...], w_ref[...])
    proj_ref[0] = res

    tm, width = res.shape
    n_heads = width // LANES
    for plane, out_ref in ((1, k_ref), (2, v_ref)):
        @pl.when(j == plane)
        def _():
            for h in range(n_heads):
                out_ref[pl.ds(h, tm, stride=n_heads), :] = res[:, h * LANES:(h + 1) * LANES]


def _mod_spec(arr, tm):
    d = arr.shape[1]
    if arr.shape[0] == 1:
        return pl.BlockSpec((1, d), lambda i, *_: (0, 0))
    return pl.BlockSpec((tm, d), lambda i, *_: (i, 0))


def _inproj(x, g, sh, sc, w_main, w_ab, width):
    t, d = x.shape
    planes = w_main.shape[1] // width
    n_heads = width // LANES
    tm = _pick_tile(t, 512)
    return pl.pallas_call(
        _inproj_kernel,
        grid=(t // tm, planes),
        in_specs=[pl.BlockSpec((tm, d), lambda i, j: (i, 0)),
                  pl.BlockSpec((1, d), lambda i, j: (0, 0)),
                  _mod_spec(sh, tm), _mod_spec(sc, tm),
                  pl.BlockSpec((d, width), lambda i, j: (0, j)),
                  pl.BlockSpec((d, LANES), lambda i, j: (0, 0))],
        out_specs=[pl.BlockSpec((1, tm, width), lambda i, j: (j, i, 0)),
                   pl.BlockSpec((tm, LANES), lambda i, j: (i, 0)),
                   pl.BlockSpec((tm * n_heads, LANES), lambda i, j: (i, 0)),
                   pl.BlockSpec((tm * n_heads, LANES), lambda i, j: (i, 0))],
        out_shape=[jax.ShapeDtypeStruct((planes, t, width), F32),
                   jax.ShapeDtypeStruct((t, LANES), F32),
                   jax.ShapeDtypeStruct((t * n_heads, LANES), F32),
                   jax.ShapeDtypeStruct((t * n_heads, LANES), F32)],
        scratch_shapes=[pltpu.VMEM((tm, d), BF16)],
        compiler_params=_cparams(("arbitrary", "arbitrary")),
        name="in_proj",
    )(x, g, sh, sc, w_main, w_ab)


def _cum_matrix(blk):
    j = lax.broadcasted_iota(jnp.int32, (blk, blk + LANES), 0)
    s = lax.broadcasted_iota(jnp.int32, (blk, blk + LANES), 1)
    return jnp.where((j >= s) | (s >= blk), 1.0, 0.0).astype(BF16)


def _sb_block(zz, v, valid, cum_mat, blk):
    y = jnp.maximum(zz, 0.0) + jnp.log2(1.0 + jnp.exp2(-jnp.abs(zz)))
    if valid is not None:
        y = jnp.where(valid, y, 0.0)
    su = _dot(y.astype(BF16), cum_mat)
    p = jnp.exp2(zz - su[:, :blk])
    if valid is not None:
        p = jnp.where(valid, p, 0.0)
    return _dot(p.astype(BF16), v), su[:, blk:]


def _sba_prompt_kernel(bias_ref, q_ref, k_ref, v_ref, o_ref, q_scr, k_scr, v_scr, acc_scr, lsum_scr,
                       *, scale2, blk, ns):
    h = pl.program_id(0)
    qi = pl.program_id(1)
    tq = ns * blk

    @pl.when(qi == 0)
    def _():
        k_scr[...] = k_ref[0].astype(BF16)
        v_scr[...] = v_ref[0].astype(BF16)

    q_scr[...] = (q_ref[0] * scale2).astype(BF16)
    acc_scr[...] = jnp.zeros_like(acc_scr)
    lsum_scr[...] = jnp.zeros_like(lsum_scr)
    bias2 = bias_ref[h] * LOG2E
    cum_mat = _cum_matrix(blk)

    def visit(kb, row0, valid):
        start = pl.multiple_of(kb * blk, blk)
        zz = _dot_nt(q_scr[row0:tq, :], k_scr[pl.ds(start, blk), :]) + bias2
        pv, tot = _sb_block(zz, v_scr[pl.ds(start, blk), :], valid, cum_mat, blk)
        lsum = lsum_scr[row0:tq, :]
        acc_scr[row0:tq, :] += jnp.exp2(-lsum) * pv
        lsum_scr[row0:tq, :] = lsum + tot

    for j in reversed(range(ns)):
        row = lax.broadcasted_iota(jnp.int32, (tq - j * blk, blk), 0)
        col = lax.broadcasted_iota(jnp.int32, (tq - j * blk, blk), 1)
        visit(qi * ns + j, j * blk, col < row)

    step = 2 if ns % 2 == 0 else 1

    def earlier(t, carry):
        for u in range(step):
            visit(qi * ns - 1 - (t * step + u), 0, None)
        return carry

    lax.fori_loop(0, qi * (ns // step), earlier, 0)
    o_ref[...] = acc_scr[...]


def _sba_prompt(proj, bias, n_heads):
    _, l, w = proj.shape
    blk = SB_BLOCK
    ns = _pick_tile(l // blk, SB_TILE_BLOCKS)
    tq = ns * blk
    kern = functools.partial(_sba_prompt_kernel, scale2=LANES ** -0.5 * LOG2E, blk=blk, ns=ns)
    return pl.pallas_call(
        kern,
        grid=(n_heads, l // tq),
        in_specs=[pl.BlockSpec(memory_space=pltpu.SMEM),
                  pl.BlockSpec((1, tq, LANES), lambda h, i: (0, i, h)),
                  pl.BlockSpec((1, l, LANES), lambda h, i: (1, 0, h)),
                  pl.BlockSpec((1, l, LANES), lambda h, i: (2, 0, h))],
        out_specs=pl.BlockSpec((tq, LANES), lambda h, i: (i, h)),
        out_shape=jax.ShapeDtypeStruct((l, w), F32),
        scratch_shapes=[pltpu.VMEM((tq, LANES), BF16), pltpu.VMEM((l, LANES), BF16), pltpu.VMEM((l, LANES), BF16),
                        pltpu.VMEM((tq, LANES), F32), pltpu.VMEM((tq, LANES), F32)],
        compiler_params=_cparams(("arbitrary", "arbitrary")),
        name="sba_prompt",
    )(bias, proj, proj, proj)


def _sba_paged_kernel(pt_ref, bias_ref, q_ref, kn_ref, vn_ref, *rest, n_heads, s, scale2, group):
    kp_refs = rest[:group]
    vp_refs = rest[group:2 * group]
    o_ref, wq_scr, bias_scr, acc_scr, lsum_scr = rest[2 * group:]
    p = pl.program_id(1)
    page = kp_refs[0].shape[1] // n_heads
    hq = n_heads * s
    w = n_heads * LANES
    cum_mat = _cum_matrix(page)

    def visit(ks, vs, valid):
        n = len(ks)
        zz = _dot_nt(wq_scr[...], jnp.concatenate(ks, axis=0))
        zz = jnp.concatenate([zz[:, g * page:(g + 1) * page] for g in range(n)], axis=0) + \
            jnp.concatenate([bias_scr[...]] * n, axis=0)
        y = jnp.maximum(zz, 0.0) + jnp.log2(1.0 + jnp.exp2(-jnp.abs(zz)))
        if valid is not None:
            y = jnp.where(valid, y, 0.0)
        su = _dot(y.astype(BF16), cum_mat)
        p = jnp.exp2(zz - su[:, :page])
        if valid is not None:
            p = jnp.where(valid, p, 0.0)
        p = p.astype(BF16)
        acc = acc_scr[...]
        lsum = lsum_scr[...]
        for g in range(n):
            pv = _dot(p[g * hq:(g + 1) * hq], vs[g])
            pv = jnp.concatenate([pv[h * s:(h + 1) * s, h * LANES:(h + 1) * LANES] for h in range(n_heads)], axis=0)
            acc = acc + jnp.exp2(-lsum) * pv
            lsum = lsum + su[g * hq:(g + 1) * hq, page:]
        acc_scr[...] = acc
        lsum_scr[...] = lsum

    @pl.when(p == 0)
    def _():
        qt = jnp.concatenate([q_ref[0] * scale2] * n_heads, axis=0)
        r2 = lax.broadcasted_iota(jnp.int32, (hq, w), 0)
        c2 = lax.broadcasted_iota(jnp.int32, (hq, w), 1)
        wq_scr[...] = jnp.where(r2 // s == c2 // LANES, qt, 0.0).astype(BF16)
        row = lax.broadcasted_iota(jnp.int32, (hq, page), 0)
        col = lax.broadcasted_iota(jnp.int32, (hq, page), 1)
        bias = jnp.zeros((hq, page), F32)
        for h in range(n_heads):
            bias = jnp.where(row // s == h, bias_ref[h] * LOG2E, bias)
        bias_scr[...] = bias
        acc_scr[...] = jnp.zeros_like(acc_scr)
        lsum_scr[...] = jnp.zeros_like(lsum_scr)
        pad = jnp.zeros((page - s, w), F32)
        kn = jnp.concatenate([kn_ref[0], pad], axis=0).astype(BF16)
        vn = jnp.concatenate([vn_ref[0], pad], axis=0).astype(BF16)
        visit([kn], [vn], col < row % s)

    def head_major(ref):
        return jnp.concatenate([ref[0, pl.ds(h, page, stride=n_heads), :] for h in range(n_heads)],
                               axis=1).astype(BF16)

    visit([head_major(r) for r in kp_refs], [head_major(r) for r in vp_refs], None)

    @pl.when(p == pl.num_programs(1) - 1)
    def _():
        for h in range(n_heads):
            o_ref[:, h * LANES:(h + 1) * LANES] = acc_scr[h * s:(h + 1) * s, :]


def _sba_paged(proj, bias, k_pool, v_pool, page_table, n_heads, s):
    _, t, w = proj.shape
    nb, n_pages = page_table.shape
    rows = k_pool.shape[1]
    page = rows // n_heads
    hq = n_heads * s
    group = _pick_tile(n_pages, 16)
    kern = functools.partial(_sba_paged_kernel, n_heads=n_heads, s=s, scale2=LANES ** -0.5 * LOG2E, group=group)

    def pool_spec(g):
        return pl.BlockSpec((1, rows, LANES), lambda b, p, pt: (pt[b, n_pages - 1 - (p * group + g)], 0, 0))

    pool_specs = [pool_spec(g) for g in range(group)]
    return pl.pallas_call(
        kern,
        grid_spec=pltpu.PrefetchScalarGridSpec(
            num_scalar_prefetch=1,
            grid=(nb, n_pages // group),
            in_specs=[pl.BlockSpec(memory_space=pltpu.SMEM),
                      pl.BlockSpec((1, s, w), lambda b, p, pt: (0, b, 0)),
                      pl.BlockSpec((1, s, w), lambda b, p, pt: (1, b, 0)),
                      pl.BlockSpec((1, s, w), lambda b, p, pt: (2, b, 0))] + pool_specs + pool_specs,
            out_specs=pl.BlockSpec((s, w), lambda b, p, pt: (b, 0)),
            scratch_shapes=[pltpu.VMEM((hq, w), BF16), pltpu.VMEM((hq, page), F32),
                            pltpu.VMEM((hq, LANES), F32), pltpu.VMEM((hq, page), F32)]),
        out_shape=jax.ShapeDtypeStruct((t, w), F32),
        compiler_params=_cparams(("arbitrary", "arbitrary")),
        name="sba_paged",
    )(page_table, bias, proj, proj, proj, *([k_pool] * group), *([v_pool] * group))


_NN = ((1,), (0,))
_NT = ((1,), (1,))


def _split(a, parts):
    out = []
    for _ in range(parts - 1):
        hi = a.astype(BF16)
        out.append(hi)
        a = a - hi.astype(F32)
    out.append(a.astype(BF16))
    return out


def _mm(a, b, dims, split):
    if not split:
        return lax.dot_general(a, b, (dims, ((), ())), preferred_element_type=F32, precision=HIGHEST)
    d = lambda x, y: lax.dot_general(x, y, (dims, ((), ())), preferred_element_type=F32)
    return d(a[1], b[0]) + d(a[0], b[1]) + d(a[0], b[0])


def _gdn_chunk(q, k, v, g_col, beta, state, c):
    heads = range(len(q))
    split = c >= 16
    sp = (lambda x: _split(x, 2)) if split else (lambda x: x)
    ri = lax.broadcasted_iota(jnp.int32, (c, c), 0)
    cj = lax.broadcasted_iota(jnp.int32, (c, c), 1)
    eye = (ri == cj).astype(F32)
    g_row = [jnp.sum(jnp.where(ri == cj, g_col[h], 0.0), axis=0, keepdims=True) for h in heads]
    if split:
        incl = (cj <= ri).astype(BF16)
        incl_t = (ri <= cj).astype(BF16)
        gc_col = [sum(_dot(incl, part) for part in reversed(_split(jnp.broadcast_to(g_col[h], (c, LANES)), 3)))
                  for h in heads]
        gc_row = [sum(_dot(part, incl_t) for part in reversed(_split(jnp.broadcast_to(g_row[h], (16, c)), 3)))[0:1]
                  for h in heads]
    else:
        gc_col = [_dot((cj <= ri).astype(F32), jnp.broadcast_to(g_col[h], (c, LANES)), HIGHEST) for h in heads]
        gc_row = [_dot(jnp.broadcast_to(g_row[h], (8, c)), (ri <= cj).astype(F32), HIGHEST)[0:1] for h in heads]
    decay = [jnp.where(cj <= ri, jnp.exp(gc_col[h][:, 0:1] - gc_row[h]), 0.0) for h in heads]
    kb = [k[h] * beta[h] for h in heads]
    k_s = [sp(k[h]) for h in heads]
    kk = [_mm(sp(kb[h]), k_s[h], _NT, split) for h in heads]
    qk = [_mm(sp(q[h]), k_s[h], _NT, split) * decay[h] for h in heads]
    n = [jnp.where(cj < ri, -kk[h] * decay[h], 0.0) for h in heads]
    inv = [eye + n[h] for h in heads]
    i = 2
    while i < c:
        n_s = [sp(n[h]) for h in heads]
        n = [_mm(n_s[h], n_s[h], _NN, split) for h in heads]
        inv = [_mm(sp(inv[h]), sp(eye + n[h]), _NN, split) for h in heads]
        i *= 2
    rhs = [jnp.concatenate([v[h] * beta[h], kb[h] * jnp.exp(gc_col[h])], axis=1) for h in heads]
    sol = [_mm(sp(inv[h]), sp(rhs[h]), _NN, split) for h in heads]
    state_s = [sp(state[h]) for h in heads]
    ws = [_mm(sp(sol[h][:, LANES:]), state_s[h], _NN, split) for h in heads]
    qs = [_mm(sp(q[h] * jnp.exp(gc_col[h])), state_s[h], _NN, split) for h in heads]
    v_new = [sol[h][:, :LANES] - ws[h] for h in heads]
    g_last = [gc_col[h][c - 1:c, :] for h in heads]
    k_dec = [k[h] * jnp.exp(g_last[h] - gc_col[h]) for h in heads]
    o = [qs[h] + _mm(sp(qk[h]), sp(v_new[h]), _NN, split) for h in heads]
    new_state = [state[h] * jnp.exp(g_last[h]) + _dot_tn(k_dec[h], v_new[h], HIGHEST) for h in heads]
    return o, new_state


def _gdn_kernel(alog_ref, dtb_ref, q_ref, k_ref, v_ref, z_ref, gab_ref, cw_ref, c0_ref, s0_ref, ng_ref,
                o_ref, s_ref, tail_scr, state_scr, *, c, n_heads, conv_w):
    ci = pl.program_id(1)
    halo = 8
    w = n_heads * LANES

    @pl.when(ci == 0)
    def _():
        state_scr[...] = s0_ref[0]
        for a in range(3):
            tail_scr[a, 0:halo, :] = jnp.zeros((halo, w), F32)
            tail_scr[a, halo - (conv_w - 1):halo, :] = c0_ref[0, a]

    def conv(a, x_ref):
        tail_scr[a, halo:halo + c, :] = x_ref[0]
        lo = halo - (conv_w - 1)
        y = sum(tail_scr[a, lo + i:lo + i + c, :] * cw_ref[i:i + 1, a * w:(a + 1) * w] for i in range(conv_w))
        tail_scr[a, 0:halo, :] = tail_scr[a, c:c + halo, :]
        return _silu(y)

    q_all = conv(0, q_ref)
    k_all = conv(1, k_ref)
    v_all = conv(2, v_ref)
    gab = gab_ref[...]
    lane = lax.broadcasted_iota(jnp.int32, gab.shape, 1)

    heads = range(n_heads)
    sl = [slice(h * LANES, (h + 1) * LANES) for h in heads]
    q = [q_all[:, sl[h]] for h in heads]
    k = [k_all[:, sl[h]] for h in heads]
    q = [q[h] * lax.rsqrt(jnp.sum(q[h] * q[h], axis=-1, keepdims=True) + EPS) * (LANES ** -0.5) for h in heads]
    k = [k[h] * lax.rsqrt(jnp.sum(k[h] * k[h], axis=-1, keepdims=True) + EPS) for h in heads]
    ga = [jnp.sum(jnp.where(lane == h, gab, 0.0), axis=-1, keepdims=True) for h in heads]
    gb = [jnp.sum(jnp.where(lane == h + n_heads, gab, 0.0), axis=-1, keepdims=True) for h in heads]
    g_col = [-jnp.exp(alog_ref[h]) * _softplus(ga[h] + dtb_ref[h]) for h in heads]
    beta = [jax.nn.sigmoid(gb[h]) for h in heads]
    o, state = _gdn_chunk(q, k, [v_all[:, sl[h]] for h in heads], g_col, beta, [state_scr[h] for h in heads], c)
    for h in heads:
        state_scr[h] = state[h]
        on = o[h] * lax.rsqrt(jnp.mean(o[h] * o[h], axis=-1, keepdims=True) + EPS) * ng_ref[...]
        o_ref[:, sl[h]] = on * _silu(z_ref[0, :, sl[h]])

    @pl.when(ci == pl.num_programs(1) - 1)
    def _():
        s_ref[0] = state_scr[...]


def _gdn(proj, gab, conv_wt, conv0, s0, a_log, dt_bias, norm_g, n_seq, c):
    _, t, w = proj.shape
    n_heads = w // LANES
    l = t // n_seq
    nc = l // c
    conv_w = conv_wt.shape[0]
    kern = functools.partial(_gdn_kernel, c=c, n_heads=n_heads, conv_w=conv_w)

    def tok(plane):
        return pl.BlockSpec((1, c, w), lambda n, ci: (plane, n * nc + ci, 0))

    smem = pl.BlockSpec(memory_space=pltpu.SMEM)
    return pl.pallas_call(
        kern,
        grid=(n_seq, nc),
        in_specs=[smem, smem, tok(3), tok(4), tok(5), tok(6),
                  pl.BlockSpec((c, LANES), lambda n, ci: (n * nc + ci, 0)),
                  pl.BlockSpec((conv_w, 3 * w), lambda n, ci: (0, 0)),
                  pl.BlockSpec((1, 3, conv_w - 1, w), lambda n, ci: (n, 0, 0, 0)),
                  pl.BlockSpec((1, n_heads, LANES, LANES), lambda n, ci: (n, 0, 0, 0)),
                  pl.BlockSpec((1, LANES), lambda n, ci: (0, 0))],
        out_specs=[pl.BlockSpec((c, w), lambda n, ci: (n * nc + ci, 0)),
                   pl.BlockSpec((1, n_heads, LANES, LANES), lambda n, ci: (n, 0, 0, 0))],
        out_shape=[jax.ShapeDtypeStruct((t, w), F32),
                   jax.ShapeDtypeStruct((n_seq, n_heads, LANES, LANES), F32)],
        scratch_shapes=[pltpu.VMEM((3, c + 8, w), F32), pltpu.VMEM((n_heads, LANES, LANES), F32)],
        compiler_params=_cparams(("arbitrary", "arbitrary")),
        name="gdn",
    )(a_log, dt_bias, proj, proj, proj, proj, gab, conv_wt, conv0, s0, norm_g)


def _outproj_kernel(osb_ref, ogdn_ref, x_ref, wa_ref, wb_ref, gt_ref, g_ref, sh_ref, sc_ref, wr_ref, br_ref,
                    x1_ref, h_ref, lg_ref):
    mix = _dot(osb_ref[...].astype(BF16), wa_ref[...]) + _dot(ogdn_ref[...].astype(BF16), wb_ref[...])
    x1 = x_ref[...] + gt_ref[...] * mix
    x1_ref[...] = x1
    y = x1 * lax.rsqrt(jnp.mean(x1 * x1, axis=-1, keepdims=True) + EPS) * g_ref[...]
    hf = y * (1.0 + sc_ref[...]) + sh_ref[...]
    h_ref[...] = hf
    lg_ref[...] = _dot(hf, wr_ref[...], HIGHEST) + br_ref[...]


def _outproj(o_sb, o_gdn, x, w_a, w_b, gt, g, sh, sc, w_r, b_r):
    t, d = x.shape
    w = o_sb.shape[1]
    tm = _pick_tile(t, 256)
    row = lambda i: (i, 0)
    fixed = lambda i: (0, 0)
    return pl.pallas_call(
        _outproj_kernel,
        grid=(t // tm,),
        in_specs=[pl.BlockSpec((tm, w), row), pl.BlockSpec((tm, w), row), pl.BlockSpec((tm, d), row),
                  pl.BlockSpec((w, d), fixed), pl.BlockSpec((w, d), fixed),
                  _mod_spec(gt, tm), pl.BlockSpec((1, d), fixed), _mod_spec(sh, tm), _mod_spec(sc, tm),
                  pl.BlockSpec((d, LANES), fixed), pl.BlockSpec((1, LANES), fixed)],
        out_specs=[pl.BlockSpec((tm, d), row), pl.BlockSpec((tm, d), row), pl.BlockSpec((tm, LANES), row)],
        out_shape=[jax.ShapeDtypeStruct((t, d), F32), jax.ShapeDtypeStruct((t, d), F32),
                   jax.ShapeDtypeStruct((t, LANES), F32)],
        compiler_params=_cparams(("arbitrary",)),
        name="out_proj",
    )(o_sb, o_gdn, x, w_a, w_b, gt, g, sh, sc, w_r, b_r)


def _expert_weights(be_ref, nx_ref, ri_ref, meta_ref, w_hbm, wbuf, w_scr, sem, col_tiles, tile):
    j = pl.program_id(0)
    i = pl.program_id(1)
    n_sweeps = pl.num_programs(0)
    active = i < meta_ref[0]
    n_runs = meta_ref[1]
    fresh = jnp.logical_and(active, jnp.logical_or(i == 0, be_ref[i] != be_ref[jnp.maximum(i - 1, 0)]))

    def copies(e, jj, slot):
        return [pltpu.make_async_copy(w_hbm.at[e, :, pl.ds(pl.multiple_of((c0 + jj) * tile, tile), tile)],
                                      wbuf.at[slot, a], sem.at[slot, a]) for a, c0 in enumerate(col_tiles)]

    @pl.when(jnp.logical_and(j == 0, i == 0))
    def _():
        for cp in copies(be_ref[0], 0, 0):
            cp.start()

    @pl.when(fresh)
    def _():
        slot = (j * n_runs + ri_ref[i]) % 2
        for cp in copies(be_ref[i], j, slot):
            cp.wait()
        nxt = nx_ref[i]
        more_runs = nxt >= 0

        @pl.when(jnp.logical_or(more_runs, j + 1 < n_sweeps))
        def _():
            for cp in copies(jnp.where(more_runs, nxt, be_ref[0]), jnp.where(more_runs, j, j + 1), 1 - slot):
                cp.start()

        for a in range(len(col_tiles)):
            w_scr[a] = wbuf[slot, a].astype(BF16)

    return active


def _moe_up_kernel(be_ref, nx_ref, ri_ref, meta_ref, x_ref, w_hbm, bg_ref, bu_ref, h_ref, wbuf, w_scr, sem, *, nf):
    tf = h_ref.shape[1]
    active = _expert_weights(be_ref, nx_ref, ri_ref, meta_ref, w_hbm, wbuf, w_scr, sem, (0, nf), tf)

    @pl.when(active)
    def _():
        x = x_ref[...].astype(BF16)
        gate = jnp.minimum(_dot(x, w_scr[0]) + bg_ref[0], SWIGLU_LIMIT)
        up = jnp.clip(_dot(x, w_scr[1]) + bu_ref[0], -SWIGLU_LIMIT, SWIGLU_LIMIT)
        h_ref[...] = ((up + 1.0) * gate * jax.nn.sigmoid(gate * SWIGLU_ALPHA)).astype(BF16)


def _moe_down_kernel(be_ref, nx_ref, ri_ref, meta_ref, h_ref, w_hbm, bd_ref, y_ref, wbuf, w_scr, sem):
    tn = y_ref.shape[1]
    active = _expert_weights(be_ref, nx_ref, ri_ref, meta_ref, w_hbm, wbuf, w_scr, sem, (0,), tn)

    @pl.when(active)
    def _():
        y_ref[...] = _dot(h_ref[...], w_scr[0]) + bd_ref[0]


def _moe_experts(xb, block_e, n_active, counts, w_gu, b_gu, w_down, b_down):
    cap, d = xb.shape
    n_blocks = cap // MOE_BLOCK
    n_exp, _, ff2 = w_gu.shape
    ff = ff2 // 2
    tf = _pick_tile(ff, 1024)
    nf = ff // tf

    has = counts > 0
    ids = jnp.arange(n_exp, dtype=jnp.int32)
    later = jnp.where((ids[None, :] > ids[:, None]) & has[None, :], ids[None, :], n_exp)
    next_nonempty = jnp.min(later, axis=1)
    next_nonempty = jnp.where(next_nonempty == n_exp, -1, next_nonempty).astype(jnp.int32)
    run_of_expert = (jnp.cumsum(has.astype(jnp.int32)) - has.astype(jnp.int32)).astype(jnp.int32)
    next_e = next_nonempty[block_e]
    run_idx = run_of_expert[block_e]
    meta = jnp.concatenate([n_active, jnp.sum(has.astype(jnp.int32)).reshape(1)]).astype(jnp.int32)

    def blk(i, meta):
        return jnp.minimum(i, meta[0] - 1)

    hbm = pl.BlockSpec(memory_space=pl.ANY)
    h = pl.pallas_call(
        functools.partial(_moe_up_kernel, nf=nf),
        grid_spec=pltpu.PrefetchScalarGridSpec(
            num_scalar_prefetch=4,
            grid=(nf, n_blocks),
            in_specs=[pl.BlockSpec((MOE_BLOCK, d), lambda j, i, be, nx, ri, meta: (blk(i, meta), 0)),
                      hbm,
                      pl.BlockSpec((1, 1, tf), lambda j, i, be, nx, ri, meta: (be[blk(i, meta)], 0, j)),
                      pl.BlockSpec((1, 1, tf), lambda j, i, be, nx, ri, meta: (be[blk(i, meta)], 0, nf + j))],
            out_specs=pl.BlockSpec((MOE_BLOCK, tf), lambda j, i, be, nx, ri, meta: (blk(i, meta), j)),
            scratch_shapes=[pltpu.VMEM((2, 2, d, tf), F32), pltpu.VMEM((2, d, tf), BF16),
                            pltpu.SemaphoreType.DMA((2, 2))]),
        out_shape=jax.ShapeDtypeStruct((cap, ff), BF16),
        compiler_params=_cparams(("arbitrary", "arbitrary")),
        name="moe_up",
    )(block_e, next_e, run_idx, meta, xb, w_gu, b_gu.reshape(n_exp, 1, ff2), b_gu.reshape(n_exp, 1, ff2))

    tn = _pick_tile(d, 2048)
    return pl.pallas_call(
        _moe_down_kernel,
        grid_spec=pltpu.PrefetchScalarGridSpec(
            num_scalar_prefetch=4,
            grid=(d // tn, n_blocks),
            in_specs=[pl.BlockSpec((MOE_BLOCK, ff), lambda j, i, be, nx, ri, meta: (blk(i, meta), 0)),
                      hbm,
                      pl.BlockSpec((1, 1, tn), lambda j, i, be, nx, ri, meta: (be[blk(i, meta)], 0, j))],
            out_specs=pl.BlockSpec((MOE_BLOCK, tn), lambda j, i, be, nx, ri, meta: (blk(i, meta), j)),
            scratch_shapes=[pltpu.VMEM((2, 1, ff, tn), F32), pltpu.VMEM((1, ff, tn), BF16),
                            pltpu.SemaphoreType.DMA((2, 1))]),
        out_shape=jax.ShapeDtypeStruct((cap, d), F32),
        compiler_params=_cparams(("arbitrary", "arbitrary")),
        name="moe_down",
    )(block_e, next_e, run_idx, meta, h, w_down, b_down.reshape(n_exp, 1, d))


def _route(logits, n_exp):
    t = logits.shape[0]
    top_logit, top_idx = lax.top_k(logits, TOP_K)
    gates = jax.nn.softmax(top_logit, axis=-1)
    n_slots = t * TOP_K
    flat_e = top_idx.reshape(-1)
    onehot = (flat_e[:, None] == jnp.arange(n_exp)[None, :]).astype(jnp.int32)
    rank = jnp.take_along_axis(jnp.cumsum(onehot, axis=0) - onehot, flat_e[:, None], axis=1)[:, 0]
    counts = jnp.sum(onehot, axis=0)
    padded = (counts + MOE_BLOCK - 1) // MOE_BLOCK * MOE_BLOCK
    pad_end = jnp.cumsum(padded)
    pad_start = pad_end - padded
    dest = (pad_start[flat_e] + rank).astype(jnp.int32)
    n_blocks = -(-n_slots // MOE_BLOCK) + n_exp
    cap = n_blocks * MOE_BLOCK
    src_tok = jnp.zeros((cap,), jnp.int32).at[dest].set(jnp.arange(n_slots, dtype=jnp.int32) // TOP_K)
    block_start = jnp.arange(n_blocks, dtype=jnp.int32) * MOE_BLOCK
    block_e = jnp.minimum(jnp.sum((block_start[:, None] >= pad_end[None, :]).astype(jnp.int32), axis=1),
                          n_exp - 1).astype(jnp.int32)
    n_active = (pad_end[-1] // MOE_BLOCK).astype(jnp.int32).reshape(1)
    return gates, dest.reshape(t, TOP_K), src_tok, block_e, n_active, counts


def _final_kernel(x_ref, y_ref, gate_ref, gt_ref, g_ref, sh_ref, sc_ref, o_ref):
    gates = gate_ref[...]
    ffn = sum(y_ref[k] * gates[:, k:k + 1] for k in range(TOP_K))
    x = x_ref[...] + gt_ref[...] * ffn
    y = x * lax.rsqrt(jnp.mean(x * x, axis=-1, keepdims=True) + EPS) * g_ref[...]
    o_ref[...] = y * (1.0 + sc_ref[...]) + sh_ref[...]


def _final(x1, y_slots, gates, row0, gt, g, sh, sc):
    t, d = x1.shape
    tm = _pick_tile(t, 256)
    assert row0 % tm == 0
    row = lambda i: (i, 0)
    return pl.pallas_call(
        _final_kernel,
        grid=(t // tm,),
        in_specs=[pl.BlockSpec((tm, d), row),
                  pl.BlockSpec((TOP_K, tm, d), lambda i: (0, row0 // tm + i, 0)),
                  pl.BlockSpec((tm, TOP_K), row), _mod_spec(gt, tm),
                  pl.BlockSpec((1, d), lambda i: (0, 0)), _mod_spec(sh, tm), _mod_spec(sc, tm)],
        out_specs=pl.BlockSpec((tm, d), row),
        out_shape=jax.ShapeDtypeStruct((t, d), F32),
        compiler_params=_cparams(("arbitrary",)),
        name="final_norm",
    )(x1, y_slots, gates, gt, g, sh, sc)


def kernel(x_prompt, x_sample, cache_k, cache_v, state_gdn, state_conv, page_table, c_prompt, c_sample, w_ada, b_ada, norm1_g, norm2_g, w_in, sb_bias, conv_w, a_log, dt_bias, gdn_norm_g, w_out, w_router, b_router, w_gu, b_gu, w_down, b_down, w_ada_final, b_ada_final, norm_f_g):
    depth = w_ada.shape[0]
    assert depth == 1, "single-layer stack"
    nbp, seq, d = x_prompt.shape
    nbs, dec_seq, _ = x_sample.shape
    assert nbp == 1
    n_sb = cache_k.shape[-2]
    n_gdn = state_gdn.shape[2]
    width = n_sb * LANES
    assert cache_k.shape[-1] == LANES and n_gdn * LANES == width
    conv_taps = conv_w.shape[-1]
    n_exp = w_router.shape[-1]
    tp, ts = nbp * seq, nbs * dec_seq

    c_all = jnp.concatenate([c_prompt, c_sample], axis=0)
    n_c = c_all.shape[0]
    c_all = jnp.pad(c_all, ((0, -n_c % 8), (0, 0)))
    mods = _ada(c_all, w_ada[0], b_ada[0])[:n_c]
    fin = _ada(c_all, w_ada_final, b_ada_final)[:n_c]
    mods_p = [m for m in jnp.split(mods[:nbp], 6, axis=-1)]
    mods_s = [jnp.repeat(m, dec_seq, axis=0) for m in jnp.split(mods[nbp:], 6, axis=-1)]
    fin_p = jnp.split(fin[:nbp], 2, axis=-1)
    fin_s = [jnp.repeat(m, dec_seq, axis=0) for m in jnp.split(fin[nbp:], 2, axis=-1)]

    w_main = w_in[0][:, :7 * width].astype(BF16)
    w_ab = jnp.pad(w_in[0][:, 7 * width:], ((0, 0), (0, LANES - 2 * n_gdn))).astype(BF16)
    g1 = norm1_g[0].reshape(1, d)
    g2 = norm2_g[0].reshape(1, d)
    conv_wt = conv_w[0].T
    w_oa = w_out[0][:width].astype(BF16)
    w_ob = w_out[0][width:].astype(BF16)
    w_r = jnp.pad(w_router[0], ((0, 0), (0, LANES - n_exp)))
    b_r = jnp.pad(b_router[0], (0, LANES - n_exp), constant_values=-1e30).reshape(1, LANES)
    ng = gdn_norm_g[0].reshape(1, LANES)

    def mixer(x, m, sba_fn, conv0, s0, n_seq, chunk):
        sh1, sc1, gt1, sh2, sc2, _ = m
        proj, gab, k_new, v_new = _inproj(x, g1, sh1, sc1, w_main, w_ab, width)
        o_sb = sba_fn(proj)
        o_gdn, s_new = _gdn(proj, gab, conv_wt, conv0, s0, a_log[0], dt_bias[0], ng, n_seq, chunk)
        x1, hffn, logits = _outproj(o_sb, o_gdn, x, w_oa, w_ob, gt1, g2, sh2, sc2, w_r, b_r)
        return (proj, k_new, v_new), s_new, x1, hffn, logits[:, :n_exp]

    def conv_layout(c0):
        n = c0.shape[0]
        return c0.reshape(n, conv_taps - 1, 3, width).transpose(0, 2, 1, 3)

    xp = x_prompt.reshape(tp, d)
    xs = x_sample.reshape(ts, d)
    conv0_p = jnp.zeros((nbp, 3, conv_taps - 1, width), F32)
    s0_p = jnp.zeros((nbp, n_gdn, LANES, LANES), F32)
    kv_p, s_p, x1_p, h_p, lg_p = mixer(
        xp, mods_p, lambda pr: _sba_prompt(pr, sb_bias[0], n_sb), conv0_p, s0_p, nbp, GDN_CHUNK)
    k_pool = cache_k.reshape(cache_k.shape[1], cache_k.shape[2] * n_sb, LANES)
    v_pool = cache_v.reshape(cache_v.shape[1], cache_v.shape[2] * n_sb, LANES)
    kv_s, s_s, x1_s, h_s, lg_s = mixer(
        xs, mods_s, lambda pr: _sba_paged(pr, sb_bias[0], k_pool, v_pool, page_table, n_sb, dec_seq),
        conv_layout(state_conv[0]), state_gdn[0], nbs, dec_seq)

    hffn = jnp.concatenate([h_p, h_s], axis=0)
    logits = jnp.concatenate([lg_p, lg_s], axis=0)
    gates, dest, src_tok, block_e, n_active, counts = _route(logits, n_exp)
    xb = hffn[src_tok]
    y = _moe_experts(xb, block_e, n_active, counts, w_gu[0], b_gu[0], w_down[0], b_down[0])
    y_slots = y[dest.T.reshape(-1)].reshape(TOP_K, tp + ts, d)

    gf = norm_f_g.reshape(1, d)
    y_p = _final(x1_p, y_slots, gates[:tp], 0, mods_p[5], gf, fin_p[0], fin_p[1])
    y_s = _final(x1_s, y_slots, gates[tp:], tp, mods_s[5], gf, fin_s[0], fin_s[1])

    def new_conv(proj, n_seq):
        planes, t, _ = proj.shape
        l = t // n_seq
        tail = lax.slice(proj.reshape(planes, n_seq, l, width), (3, 0, l - (conv_taps - 1), 0), (6, n_seq, l, width))
        return tail.transpose(1, 2, 0, 3).reshape(n_seq, conv_taps - 1, 3 * width)

    heads = lambda a, n, l: a.reshape(1, n, l, n_sb, LANES)
    return (y_p.reshape(nbp, seq, d), y_s.reshape(nbs, dec_seq, d),
            heads(kv_p[1], nbp, seq), heads(kv_p[2], nbp, seq),
            heads(kv_s[1], nbs, dec_seq), heads(kv_s[2], nbs, dec_seq),
            s_p[None], s_s[None], new_conv(kv_p[0], nbp)[None], new_conv(kv_s[0], nbs)[None])
```
